```python
import math
import jax, jax.numpy as jnp
from jax import lax
import numpy as np

D_MODEL = 1024
BATCH = 16
SEQ = 2048
DEPTH = 4

MIX_WIDTH = D_MODEL
DN_HEADS = 8
DN_HEAD_DIM = 64
DN_WIDTH = DN_HEADS * DN_HEAD_DIM
DN_CONV = 4
CHUNK = 64
SC_WIDTH = MIX_WIDTH - DN_WIDTH
SC_GROUPS = 8
SC_GROUP_DIM = SC_WIDTH // SC_GROUPS
SC_CONV = 3
D_FF = ((8 * D_MODEL // 3 + 127) // 128) * 128
FFN_CONV = 3
EPS = 1e-6
IN_COLS = 4 * DN_WIDTH + 2 * DN_HEADS + 3 * SC_WIDTH
IN_SPLITS = (3 * DN_WIDTH,
             4 * DN_WIDTH,
             4 * DN_WIDTH + DN_HEADS,
             4 * DN_WIDTH + 2 * DN_HEADS,
             4 * DN_WIDTH + 2 * DN_HEADS + SC_WIDTH,
             4 * DN_WIDTH + 2 * DN_HEADS + 2 * SC_WIDTH)

kernel_name = "hybrid_deltanet_shortconv_convffn"


def rms_norm(x, gain):
    xf = x.astype(jnp.float32)
    y = xf * lax.rsqrt(jnp.mean(xf * xf, axis=-1, keepdims=True) + EPS)
    return (y * gain.astype(jnp.float32)).astype(x.dtype)


def l2_norm(x):
    return x * lax.rsqrt(jnp.sum(x * x, axis=-1, keepdims=True) + EPS)


def causal_dwconv(x, w):
    K, C = w.shape
    return lax.conv_general_dilated(
        x, w[:, None, :].astype(x.dtype), window_strides=(1,), padding=[(K - 1, 0)],
        dimension_numbers=("NWC", "WIO", "NWC"), feature_group_count=C)


def gated_delta_rule(q, k, v, g, beta):
    Bsz, T, H, Dk = q.shape
    Dv = v.shape[-1]
    N = T // CHUNK

    def chunks(t):
        t = jnp.moveaxis(t, 2, 1)
        return t.reshape((Bsz, H, N, CHUNK) + t.shape[3:])

    q, k, v, g, beta = (chunks(t) for t in (q, k, v, g, beta))
    g = jnp.cumsum(g, axis=-1)
    causal = jnp.tril(jnp.ones((CHUNK, CHUNK), bool))
    strict = jnp.tril(jnp.ones((CHUNK, CHUNK), bool), -1)
    decay = jnp.exp(jnp.where(causal, g[..., :, None] - g[..., None, :], -jnp.inf))

    k_beta = k * beta[..., None]
    v_beta = v * beta[..., None]
    a_mat = jnp.where(strict, jnp.einsum("bhncd,bhnsd->bhncs", k_beta, k) * decay, 0.0)
    tri = a_mat + jnp.eye(CHUNK, dtype=a_mat.dtype)
    u = lax.linalg.triangular_solve(tri, v_beta, left_side=True, lower=True, unit_diagonal=True)
    w = lax.linalg.triangular_solve(tri, k_beta * jnp.exp(g)[..., None],
                                    left_side=True, lower=True, unit_diagonal=True)

    qk = jnp.where(causal, jnp.einsum("bhncd,bhnsd->bhncs", q, k) * decay, 0.0)
    q_dec = q * jnp.exp(g)[..., None]
    g_last = g[..., -1]
    k_dec = k * jnp.exp(g_last[..., None] - g)[..., None]

    def step(S, xs):
        qd_c, qk_c, u_c, w_c, kd_c, gl_c = xs
        v_new = u_c - jnp.einsum("bhck,bhkv->bhcv", w_c, S)
        o_c = jnp.einsum("bhck,bhkv->bhcv", qd_c, S) + jnp.einsum("bhcs,bhsv->bhcv", qk_c, v_new)
        S = S * jnp.exp(gl_c)[..., None, None] + jnp.einsum("bhck,bhcv->bhkv", kd_c, v_new)
        return S, o_c

    xs = tuple(jnp.moveaxis(t, 2, 0) for t in (q_dec, qk, u, w, k_dec, g_last))
    S0 = jnp.zeros((Bsz, H, Dk, Dv), jnp.float32)
    _, o = lax.scan(step, S0, xs)
    o = jnp.moveaxis(o, 0, 2).reshape(Bsz, H, T, Dv)
    return jnp.moveaxis(o, 1, 2)


def hybrid_mixer(h, w_in, conv_qkv, a_log, dt_bias, head_norm, conv_sc, sc_norm, w_out):
    Bsz, T, _ = h.shape
    f32 = jnp.float32
    proj = jnp.einsum("btd,de->bte", h, w_in)
    qkv, z, b, a, sc_b, sc_c, sc_x = jnp.split(proj, IN_SPLITS, axis=-1)

    qkv = jax.nn.silu(causal_dwconv(qkv, conv_qkv))
    q, k, v = jnp.split(qkv, 3, axis=-1)
    heads = lambda t: t.reshape(Bsz, T, DN_HEADS, DN_HEAD_DIM).astype(f32)
    q = l2_norm(heads(q)) * (DN_HEAD_DIM ** -0.5)
    k = l2_norm(heads(k))
    v = heads(v)
    beta = jax.nn.sigmoid(b.astype(f32))
    g = -jnp.exp(a_log.astype(f32)) * jax.nn.softplus(a.astype(f32) + dt_bias.astype(f32))
    o = gated_delta_rule(q, k, v, g, beta)
    o = rms_norm(o, head_norm) * jax.nn.silu(heads(z))
    o_dn = o.reshape(Bsz, T, DN_WIDTH).astype(h.dtype)

    y = sc_b * causal_dwconv(sc_c * sc_x, conv_sc)
    y = rms_norm(y.reshape(Bsz, T, SC_GROUPS, SC_GROUP_DIM), sc_norm).reshape(Bsz, T, SC_WIDTH)

    mixed = jnp.concatenate([o_dn, y.astype(h.dtype)], axis=-1)
    return jnp.einsum("bte,ed->btd", mixed, w_out)


def channel_mixer(h, w_up, conv_ffn, w_down):
    up = jnp.einsum("btd,df->btf", h, w_up)
    val, gate = jnp.split(up, 2, axis=-1)
    val = causal_dwconv(val, conv_ffn)
    return jnp.einsum("btf,fd->btd", jax.nn.silu(val) * gate, w_down)


def setup_inputs(seed: int = 0) -> dict:
    key = jax.random.key(seed)
    ks = jax.random.split(key, 16)
    f32 = jnp.float32
    nrm = lambda k, shape, scale: scale * jax.random.normal(k, shape, f32)
    gain = lambda k, shape: 1.0 + 0.02 * jax.random.normal(k, shape, f32)
    x = jax.random.normal(ks[0], (BATCH, SEQ, D_MODEL), f32)
    attn_norm = gain(ks[1], (DEPTH, D_MODEL))
    w_in = nrm(ks[2], (DEPTH, D_MODEL, IN_COLS), D_MODEL ** -0.5)
    conv_qkv = nrm(ks[3], (DEPTH, DN_CONV, 3 * DN_WIDTH), DN_CONV ** -0.5)
    a_log = jnp.log(jax.random.uniform(ks[4], (DEPTH, DN_HEADS), f32, 1.0, 16.0))
    dt = jnp.exp(jax.random.uniform(ks[5], (DEPTH, DN_HEADS), f32, math.log(1e-3), math.log(1e-1)))
    dt_bias = dt + jnp.log(-jnp.expm1(-dt))
    head_norm = gain(ks[6], (DEPTH, DN_HEAD_DIM))
    conv_sc = nrm(ks[7], (DEPTH, SC_CONV, SC_WIDTH), SC_CONV ** -0.5)
    sc_norm = gain(ks[8], (DEPTH, SC_GROUPS, SC_GROUP_DIM))
    w_out = nrm(ks[9], (DEPTH, MIX_WIDTH, D_MODEL), MIX_WIDTH ** -0.5)
    ffn_norm = gain(ks[10], (DEPTH, D_MODEL))
    w_up = nrm(ks[11], (DEPTH, D_MODEL, 2 * D_FF), D_MODEL ** -0.5)
    conv_ffn = nrm(ks[12], (DEPTH, FFN_CONV, D_FF), FFN_CONV ** -0.5)
    w_down = nrm(ks[13], (DEPTH, D_FF, D_MODEL), D_FF ** -0.5)
    final_norm = gain(ks[14], (D_MODEL,))
    return {"x": x, "attn_norm": attn_norm, "w_in": w_in, "conv_qkv": conv_qkv,
            "a_log": a_log, "dt_bias": dt_bias, "head_norm": head_norm, "conv_sc": conv_sc,
            "sc_norm": sc_norm, "w_out": w_out, "ffn_norm": ffn_norm, "w_up": w_up,
            "conv_ffn": conv_ffn, "w_down": w_down, "final_norm": final_norm}


def reference(x, attn_norm, w_in, conv_qkv, a_log, dt_bias, head_norm, conv_sc, sc_norm,
              w_out, ffn_norm, w_up, conv_ffn, w_down, final_norm):
    for l in range(DEPTH):
        x = x + hybrid_mixer(rms_norm(x, attn_norm[l]), w_in[l], conv_qkv[l], a_log[l], dt_bias[l],
                             head_norm[l], conv_sc[l], sc_norm[l], w_out[l])
        x = x + channel_mixer(rms_norm(x, ffn_norm[l]), w_up[l], conv_ffn[l], w_down[l])
    return rms_norm(x, final_norm)
```

```python
import functools

import jax
import jax.numpy as jnp
from jax import lax
from jax.experimental import pallas as pl
from jax.experimental.pallas import tpu as pltpu

F32 = jnp.float32
BF16 = jnp.bfloat16

D_MODEL = 1024
DN_HEADS = 8
HEAD_DIM = 64
DN_WIDTH = DN_HEADS * HEAD_DIM
SC_WIDTH = D_MODEL - DN_WIDTH
QKV_WIDTH = 3 * DN_WIDTH
D_FF = 2816
CHUNK = 64
EPS = 1e-6
LANES = 128
N_PAIRS = DN_HEADS // 2
MAIN_COLS = QKV_WIDTH + DN_WIDTH + 3 * SC_WIDTH
Z_OFF = QKV_WIDTH
SCB_OFF = Z_OFF + DN_WIDTH
SCC_OFF = SCB_OFF + SC_WIDTH
SCX_OFF = SCC_OFF + SC_WIDTH
HALO = 8
FF_BLK = 256
NEG_BIG = -1e30
VMEM_LIMIT = 56 * 1024 * 1024


def _mm(a, b):
    return jnp.dot(a.astype(BF16), b.astype(BF16), preferred_element_type=F32)


def _mm_nt(a, b):
    return lax.dot_general(a.astype(BF16), b.astype(BF16), (((1,), (1,)), ((), ())),
                           preferred_element_type=F32)


def _rms(x, gain):
    return x * lax.rsqrt(jnp.mean(x * x, axis=-1, keepdims=True) + EPS) * gain


def _silu(x):
    return x * jax.nn.sigmoid(x)


def _group_ones():
    r = lax.broadcasted_iota(jnp.int32, (LANES, LANES), 0) // HEAD_DIM
    c = lax.broadcasted_iota(jnp.int32, (LANES, LANES), 1) // HEAD_DIM
    return (r == c).astype(BF16)


def _group_sum(x, ones2):
    return jnp.dot(x.astype(BF16), ones2, preferred_element_type=F32)


def _inproj_kernel(x_ref, gain_ref, w_ref, wba_ref, proj_ref, ba_ref):
    hb = _rms(x_ref[...], gain_ref[...]).astype(BF16)
    for j in range(0, MAIN_COLS, 512):
        proj_ref[:, j:j + 512] = jnp.dot(hb, w_ref[:, j:j + 512],
                                         preferred_element_type=F32).astype(BF16)
    ba_ref[...] = jnp.dot(hb, wba_ref[...], preferred_element_type=F32)


def _inproj(x2d, gain, w_main, w_ba, tm):
    m = x2d.shape[0]
    const = lambda i: (0, 0)
    return pl.pallas_call(
        _inproj_kernel,
        grid=(m // tm,),
        in_specs=[
            pl.BlockSpec((tm, D_MODEL), lambda i: (i, 0)),
            pl.BlockSpec((1, D_MODEL), const),
            pl.BlockSpec((D_MODEL, MAIN_COLS), const, pipeline_mode=pl.Buffered(1)),
            pl.BlockSpec((D_MODEL, LANES), const, pipeline_mode=pl.Buffered(1)),
        ],
        out_specs=[
            pl.BlockSpec((tm, MAIN_COLS), lambda i: (i, 0)),
            pl.BlockSpec((tm, LANES), lambda i: (i, 0)),
        ],
        out_shape=[
            jax.ShapeDtypeStruct((m, MAIN_COLS), BF16),
            jax.ShapeDtypeStruct((m, LANES), F32),
        ],
        compiler_params=pltpu.CompilerParams(
            dimension_semantics=("parallel",), vmem_limit_bytes=VMEM_LIMIT),
        name="inproj",
    )(x2d, gain, w_main, w_ba)


def _mixer_kernel(proj_ref, ba_ref, wq_ref, wsc_ref, hn_ref, scn_ref, alog_ref, dtb_ref,
                  out_ref, cq, cs, qs, ks, vs, bs, gs, osc, s_ref, *, ts):
    t = pl.program_id(1)

    @pl.when(t == 0)
    def _():
        cq[0:HALO, :] = jnp.zeros((HALO, QKV_WIDTH), F32)
        cs[0:HALO, :] = jnp.zeros((HALO, SC_WIDTH), F32)
        s_ref[...] = jnp.zeros_like(s_ref)

    ones2 = _group_ones()

    cq[HALO:HALO + ts, :] = proj_ref[:, 0:QKV_WIDTH].astype(F32)
    for part, dst in enumerate((qs, ks, vs)):
        for p in range(N_PAIRS):
            lo = part * DN_WIDTH + p * LANES
            acc = cq[HALO:HALO + ts, lo:lo + LANES] * wq_ref[3:4, lo:lo + LANES]
            for j in range(1, 4):
                acc = acc + cq[HALO - j:HALO - j + ts, lo:lo + LANES] * wq_ref[3 - j:4 - j, lo:lo + LANES]
            y = _silu(acc)
            if part < 2:
                y = y * lax.rsqrt(_group_sum(y * y, ones2) + EPS)
                if part == 0:
                    y = y * (HEAD_DIM ** -0.5)
            dst[:, p * LANES:(p + 1) * LANES] = y
    cq[0:HALO, :] = cq[ts:ts + HALO, :]

    ba = ba_ref[...]
    bs[...] = jax.nn.sigmoid(ba)
    sp_in = ba + dtb_ref[...]
    softplus = jnp.maximum(sp_in, 0.0) + jnp.log1p(jnp.exp(-jnp.abs(sp_in)))
    g_raw = -jnp.exp(alog_ref[...]) * softplus
    rt = lax.broadcasted_iota(jnp.int32, (ts, ts), 0)
    ct = lax.broadcasted_iota(jnp.int32, (ts, ts), 1)
    tri = ((rt >= ct) & (rt // CHUNK == ct // CHUNK)).astype(BF16)
    g_hi = g_raw.astype(BF16)
    g_r1 = g_raw - g_hi.astype(F32)
    g_mid = g_r1.astype(BF16)
    g_lo = (g_r1 - g_mid.astype(F32)).astype(BF16)
    gs[...] = (jnp.dot(tri, g_hi, preferred_element_type=F32)
               + jnp.dot(tri, g_mid, preferred_element_type=F32)
               + jnp.dot(tri, g_lo, preferred_element_type=F32))

    row = lax.broadcasted_iota(jnp.int32, (LANES, LANES), 0)
    col = lax.broadcasted_iota(jnp.int32, (LANES, LANES), 1)
    same_head = (row // CHUNK) == (col // CHUNK)
    causal = same_head & (row >= col)
    strict = same_head & (row > col)
    blk16 = (row // 16) == (col // 16)
    eye = (row == col).astype(F32)
    lane_lo = lax.broadcasted_iota(jnp.int32, (CHUNK, LANES), 1) < HEAD_DIM

    def stack(x):
        return jnp.concatenate([jnp.where(lane_lo, x, 0.0), jnp.where(lane_lo, 0.0, x)], axis=0)

    def rep_cols(x, c1, c2):
        return jnp.concatenate([jnp.broadcast_to(x[:, c1:c1 + 1], (CHUNK, LANES)),
                                jnp.broadcast_to(x[:, c2:c2 + 1], (CHUNK, LANES))], axis=0)

    def chunk_body(c, carry):
        r0 = pl.multiple_of(c * CHUNK, CHUNK)
        bcols = bs[pl.ds(r0, CHUNK), :]
        gcols = gs[pl.ds(r0, CHUNK), :]
        for p in range(N_PAIRS):
            h1, h2 = 2 * p, 2 * p + 1
            sl = slice(p * LANES, (p + 1) * LANES)
            brep = rep_cols(bcols, h1, h2)
            grep = rep_cols(gcols, DN_HEADS + h1, DN_HEADS + h2)
            kk = stack(ks[pl.ds(r0, CHUNK), sl])
            qq = stack(qs[pl.ds(r0, CHUNK), sl])
            vv = stack(vs[pl.ds(r0, CHUNK), sl])
            eg = jnp.exp(grep)
            glrep = jnp.concatenate(
                [jnp.broadcast_to(grep[CHUNK - 1:CHUNK, :], (CHUNK, LANES)),
                 jnp.broadcast_to(grep[2 * CHUNK - 1:2 * CHUNK, :], (CHUNK, LANES))], axis=0)
            kb = kk * brep
            vb = vv * brep
            decay = jnp.exp(jnp.where(causal, grep - grep.T, NEG_BIG))
            k_bf = kk.astype(BF16)
            a_mat = jnp.where(strict, _mm_nt(kb, k_bf) * decay, 0.0)
            qk = _mm_nt(qq, k_bf) * decay
            bm = jnp.where(blk16, -a_mat, 0.0)
            lm = jnp.where(blk16, 0.0, a_mat)
            b2 = _mm(bm, bm)
            b4 = _mm(b2, b2)
            b8 = _mm(b4, b4)
            dinv = eye + bm
            dinv = dinv + _mm(dinv, b2)
            dinv = dinv + _mm(dinv, b4)
            dinv = dinv + _mm(dinv, b8)
            nm = -_mm(dinv, lm)
            n2 = _mm(nm, nm)
            rhs = jnp.concatenate([vb, kb * eg], axis=1)
            y = _mm(dinv, rhs)
            y = y + _mm(n2, y)
            y = y + _mm(nm, y)
            u = y[:, 0:LANES]
            w = y[:, LANES:2 * LANES]
            s_old = s_ref[p]
            s_bf = s_old.astype(BF16)
            v_new = u - _mm(w, s_bf)
            o2 = _mm(qq * eg, s_bf) + _mm(qk, v_new)
            k_dec = kk * jnp.exp(glrep - grep)
            s_ref[p] = s_old * jnp.exp(glrep) + _mm(k_dec.T, v_new)
            osc[pl.ds(r0, CHUNK), sl] = o2[0:CHUNK, :] + o2[CHUNK:2 * CHUNK, :]
        return carry

    lax.fori_loop(0, ts // CHUNK, chunk_body, 0)

    for p in range(N_PAIRS):
        sl = slice(p * LANES, (p + 1) * LANES)
        o = osc[:, sl]
        ms = _group_sum(o * o, ones2) * (1.0 / HEAD_DIM)
        z = proj_ref[:, Z_OFF + p * LANES:Z_OFF + (p + 1) * LANES].astype(F32)
        out_ref[:, sl] = (o * lax.rsqrt(ms + EPS) * hn_ref[:, sl] * _silu(z)).astype(out_ref.dtype)

    cs[HALO:HALO + ts, :] = (proj_ref[:, SCC_OFF:SCC_OFF + SC_WIDTH].astype(F32)
                             * proj_ref[:, SCX_OFF:SCX_OFF + SC_WIDTH].astype(F32))
    for p in range(SC_WIDTH // LANES):
        sl = slice(p * LANES, (p + 1) * LANES)
        acc = cs[HALO:HALO + ts, sl] * wsc_ref[2:3, sl]
        for j in range(1, 3):
            acc = acc + cs[HALO - j:HALO - j + ts, sl] * wsc_ref[2 - j:3 - j, sl]
        y = acc * proj_ref[:, SCB_OFF + p * LANES:SCB_OFF + (p + 1) * LANES].astype(F32)
        ms = _group_sum(y * y, ones2) * (1.0 / HEAD_DIM)
        out_ref[:, DN_WIDTH + p * LANES:DN_WIDTH + (p + 1) * LANES] = (
            y * lax.rsqrt(ms + EPS) * scn_ref[:, sl]).astype(out_ref.dtype)
    cs[0:HALO, :] = cs[ts:ts + HALO, :]


def _mixer(proj, ba, wq, wsc, hn, scn, alog, dtb, ts):
    b, t, _ = proj.shape
    const = lambda i, j: (0, 0)
    return pl.pallas_call(
        functools.partial(_mixer_kernel, ts=ts),
        grid=(b, t // ts),
        in_specs=[
            pl.BlockSpec((None, ts, MAIN_COLS), lambda i, j: (i, j, 0)),
            pl.BlockSpec((None, ts, LANES), lambda i, j: (i, j, 0)),
            pl.BlockSpec((4, QKV_WIDTH), const),
            pl.BlockSpec((3, SC_WIDTH), const),
            pl.BlockSpec((1, DN_WIDTH), const),
            pl.BlockSpec((1, SC_WIDTH), const),
            pl.BlockSpec((1, LANES), const),
            pl.BlockSpec((1, LANES), const),
        ],
        out_specs=pl.BlockSpec((None, ts, D_MODEL), lambda i, j: (i, j, 0)),
        out_shape=jax.ShapeDtypeStruct((b, t, D_MODEL), BF16),
        scratch_shapes=[
            pltpu.VMEM((HALO + ts, QKV_WIDTH), F32),
            pltpu.VMEM((HALO + ts, SC_WIDTH), F32),
            pltpu.VMEM((ts, DN_WIDTH), F32),
            pltpu.VMEM((ts, DN_WIDTH), F32),
            pltpu.VMEM((ts, DN_WIDTH), F32),
            pltpu.VMEM((ts, LANES), F32),
            pltpu.VMEM((ts, LANES), F32),
            pltpu.VMEM((ts, DN_WIDTH), F32),
            pltpu.VMEM((N_PAIRS, LANES, LANES), F32),
        ],
        compiler_params=pltpu.CompilerParams(
            dimension_semantics=("parallel", "arbitrary"), vmem_limit_bytes=VMEM_LIMIT),
        name="mixer",
    )(proj, ba, wq, wsc, hn, scn, alog, dtb)


def _ffn_kernel(x_ref, mixed_ref, wout_ref, gain_ref, wup_ref, cw_ref, wdown_ref, fin_ref,
                out_ref, work, halo, *, tm, final):
    t = pl.program_id(1)

    @pl.when(t == 0)
    def _():
        halo[...] = jnp.zeros_like(halo)

    x1 = x_ref[...] + jnp.dot(mixed_ref[...], wout_ref[...], preferred_element_type=F32)
    hb = _rms(x1, gain_ref[...]).astype(BF16)
    acc = jnp.zeros((tm, D_MODEL), F32)
    for j in range(D_FF // FF_BLK):
        sl = slice(j * FF_BLK, (j + 1) * FF_BLK)
        val = jnp.dot(hb, wup_ref[:, sl], preferred_element_type=F32)
        gate = jnp.dot(hb, wup_ref[:, D_FF + j * FF_BLK:D_FF + (j + 1) * FF_BLK],
                       preferred_element_type=F32)
        work[0:HALO, :] = halo[:, sl]
        work[HALO:HALO + tm, :] = val
        cv = (val * cw_ref[2:3, sl] + work[HALO - 1:HALO - 1 + tm, :] * cw_ref[1:2, sl]
              + work[HALO - 2:HALO - 2 + tm, :] * cw_ref[0:1, sl])
        halo[:, sl] = work[tm:tm + HALO, :]
        act = (_silu(cv) * gate).astype(BF16)
        acc = acc + jnp.dot(act, wdown_ref[sl, :], preferred_element_type=F32)
    x2 = x1 + acc
    if final:
        x2 = _rms(x2, fin_ref[...])
    out_ref[...] = x2


def _ffn(x, mixed, w_out, gain, w_up, cw, w_down, fin, tm, final):
    b, t, _ = x.shape
    const = lambda i, j: (0, 0)
    once = pl.Buffered(1)
    return pl.pallas_call(
        functools.partial(_ffn_kernel, tm=tm, final=final),
        grid=(b, t // tm),
        in_specs=[
            pl.BlockSpec((None, tm, D_MODEL), lambda i, j: (i, j, 0)),
            pl.BlockSpec((None, tm, D_MODEL), lambda i, j: (i, j, 0)),
            pl.BlockSpec((D_MODEL, D_MODEL), const, pipeline_mode=once),
            pl.BlockSpec((1, D_MODEL), const),
            pl.BlockSpec((D_MODEL, 2 * D_FF), const, pipeline_mode=once),
            pl.BlockSpec((3, D_FF), const),
            pl.BlockSpec((D_FF, D_MODEL), const, pipeline_mode=once),
            pl.BlockSpec((1, D_MODEL), const),
        ],
        out_specs=pl.BlockSpec((None, tm, D_MODEL), lambda i, j: (i, j, 0)),
        out_shape=jax.ShapeDtypeStruct((b, t, D_MODEL), F32),
        scratch_shapes=[
            pltpu.VMEM((HALO + tm, FF_BLK), F32),
            pltpu.VMEM((HALO, D_FF), F32),
        ],
        compiler_params=pltpu.CompilerParams(
            dimension_semantics=("parallel", "arbitrary"), vmem_limit_bytes=VMEM_LIMIT),
        name="ffn",
    )(x, mixed, w_out, gain, w_up, cw, w_down, fin)


def kernel(x, attn_norm, w_in, conv_qkv, a_log, dt_bias, head_norm, conv_sc, sc_norm, w_out,
           ffn_norm, w_up, conv_ffn, w_down, final_norm):
    b, t, d = x.shape
    depth = w_in.shape[0]
    ba_lo = QKV_WIDTH + DN_WIDTH
    ba_hi = ba_lo + 2 * DN_HEADS
    fin = final_norm.reshape(1, d)
    for l in range(depth):
        w = w_in[l]
        w_main = jnp.concatenate([w[:, :ba_lo], w[:, ba_hi:]], axis=1).astype(BF16)
        w_ba = jnp.pad(w[:, ba_lo:ba_hi], ((0, 0), (0, LANES - 2 * DN_HEADS))).astype(BF16)
        proj, ba = _inproj(x.reshape(b * t, d), attn_norm[l].reshape(1, d), w_main, w_ba, tm=512)
        alog = jnp.pad(a_log[l], (DN_HEADS, LANES - 2 * DN_HEADS)).reshape(1, LANES)
        dtb = jnp.pad(dt_bias[l], (DN_HEADS, LANES - 2 * DN_HEADS)).reshape(1, LANES)
        mixed = _mixer(proj.reshape(b, t, MAIN_COLS), ba.reshape(b, t, LANES),
                       conv_qkv[l], conv_sc[l],
                       jnp.tile(head_norm[l], DN_HEADS).reshape(1, DN_WIDTH),
                       sc_norm[l].reshape(1, SC_WIDTH), alog, dtb, ts=256)
        x = _ffn(x, mixed, w_out[l].astype(BF16), ffn_norm[l].reshape(1, d),
                 w_up[l].astype(BF16), conv_ffn[l], w_down[l].astype(BF16), fin,
                 tm=512, final=(l == depth - 1))
    return x
```

```python
import functools

import jax
import jax.numpy as jnp
from jax import lax
from jax.experimental import pallas as pl
from jax.experimental.pallas import tpu as pltpu

F32 = jnp.float32
BF16 = jnp.bfloat16

D_MODEL = 1024
DN_HEADS = 8
HEAD_DIM = 64
DN_WIDTH = DN_HEADS * HEAD_DIM
SC_WIDTH = D_MODEL - DN_WIDTH
QKV_WIDTH = 3 * DN_WIDTH
D_FF = 2816
CHUNK = 64
EPS = 1e-6
LANES = 128
N_PAIRS = DN_HEADS // 2
MAIN_COLS = QKV_WIDTH + DN_WIDTH + 3 * SC_WIDTH
Z_OFF = QKV_WIDTH
SCB_OFF = Z_OFF + DN_WIDTH
SCC_OFF = SCB_OFF + SC_WIDTH
SCX_OFF = SCC_OFF + SC_WIDTH
HALO = 8
FF_BLK = 256
NEG_BIG = -1e30
CHUNKS_PER_GROUP = 2
VMEM_LIMIT = 56 * 1024 * 1024


def _mm(a, b):
    return jnp.dot(a.astype(BF16), b.astype(BF16), preferred_element_type=F32)


def _mm_nt(a, b):
    return lax.dot_general(a.astype(BF16), b.astype(BF16), (((1,), (1,)), ((), ())),
                           preferred_element_type=F32)


def _rms(x, gain):
    return x * lax.rsqrt(jnp.mean(x * x, axis=-1, keepdims=True) + EPS) * gain


def _silu(x):
    return x * jax.nn.sigmoid(x)


def _group_ones():
    r = lax.broadcasted_iota(jnp.int32, (LANES, LANES), 0) // HEAD_DIM
    c = lax.broadcasted_iota(jnp.int32, (LANES, LANES), 1) // HEAD_DIM
    return (r == c).astype(BF16)


def _group_sum(x, ones2):
    return jnp.dot(x.astype(BF16), ones2, preferred_element_type=F32)


def _inproj_kernel(x_ref, gain_ref, w_ref, wba_ref, proj_ref, ba_ref):
    hb = _rms(x_ref[...], gain_ref[...]).astype(BF16)
    for j in range(0, MAIN_COLS, 512):
        proj_ref[:, j:j + 512] = jnp.dot(hb, w_ref[:, j:j + 512],
                                         preferred_element_type=F32).astype(BF16)
    ba_ref[...] = jnp.dot(hb, wba_ref[...], preferred_element_type=F32)


def _inproj(x2d, gain, w_main, w_ba, tm):
    m = x2d.shape[0]
    const = lambda i: (0, 0)
    return pl.pallas_call(
        _inproj_kernel,
        grid=(m // tm,),
        in_specs=[
            pl.BlockSpec((tm, D_MODEL), lambda i: (i, 0)),
            pl.BlockSpec((1, D_MODEL), const),
            pl.BlockSpec((D_MODEL, MAIN_COLS), const, pipeline_mode=pl.Buffered(1)),
            pl.BlockSpec((D_MODEL, LANES), const, pipeline_mode=pl.Buffered(1)),
        ],
        out_specs=[
            pl.BlockSpec((tm, MAIN_COLS), lambda i: (i, 0)),
            pl.BlockSpec((tm, LANES), lambda i: (i, 0)),
        ],
        out_shape=[
            jax.ShapeDtypeStruct((m, MAIN_COLS), BF16),
            jax.ShapeDtypeStruct((m, LANES), F32),
        ],
        compiler_params=pltpu.CompilerParams(
            dimension_semantics=("parallel",), vmem_limit_bytes=VMEM_LIMIT),
        name="inproj",
    )(x2d, gain, w_main, w_ba)


def _mixer_kernel(proj_ref, ba_ref, cwq_ref, wsc_ref, hn_ref, scn_ref, alog_ref, dtb_ref,
                  out_ref, cq, cs, qs, ks, vs, bs, gs, osc, s_ref,
                  egl_ref, c_ref, u_ref, p_ref, qk_ref, sp_ref, wq_ref, *, ts):
    t = pl.program_id(1)

    @pl.when(t == 0)
    def _():
        cq[0:HALO, :] = jnp.zeros((HALO, QKV_WIDTH), F32)
        cs[0:HALO, :] = jnp.zeros((HALO, SC_WIDTH), F32)
        s_ref[...] = jnp.zeros_like(s_ref)

    ones2 = _group_ones()

    cq[HALO:HALO + ts, :] = proj_ref[:, 0:QKV_WIDTH].astype(F32)
    for part, dst in enumerate((qs, ks, vs)):
        for p in range(N_PAIRS):
            lo = part * DN_WIDTH + p * LANES
            acc = cq[HALO:HALO + ts, lo:lo + LANES] * cwq_ref[3:4, lo:lo + LANES]
            for j in range(1, 4):
                acc = acc + cq[HALO - j:HALO - j + ts, lo:lo + LANES] * cwq_ref[3 - j:4 - j, lo:lo + LANES]
            y = _silu(acc)
            if part < 2:
                y = y * lax.rsqrt(_group_sum(y * y, ones2) + EPS)
                if part == 0:
                    y = y * (HEAD_DIM ** -0.5)
            dst[:, p * LANES:(p + 1) * LANES] = y
    cq[0:HALO, :] = cq[ts:ts + HALO, :]

    ba = ba_ref[...]
    bs[...] = jax.nn.sigmoid(ba)
    sp_in = ba + dtb_ref[...]
    softplus = jnp.maximum(sp_in, 0.0) + jnp.log1p(jnp.exp(-jnp.abs(sp_in)))
    g_raw = -jnp.exp(alog_ref[...]) * softplus
    rt = lax.broadcasted_iota(jnp.int32, (ts, ts), 0)
    ct = lax.broadcasted_iota(jnp.int32, (ts, ts), 1)
    tri = ((rt >= ct) & (rt // CHUNK == ct // CHUNK)).astype(BF16)
    g_hi = g_raw.astype(BF16)
    g_r1 = g_raw - g_hi.astype(F32)
    g_mid = g_r1.astype(BF16)
    g_lo = (g_r1 - g_mid.astype(F32)).astype(BF16)
    gs[...] = (jnp.dot(tri, g_hi, preferred_element_type=F32)
               + jnp.dot(tri, g_mid, preferred_element_type=F32)
               + jnp.dot(tri, g_lo, preferred_element_type=F32))

    row = lax.broadcasted_iota(jnp.int32, (LANES, LANES), 0)
    col = lax.broadcasted_iota(jnp.int32, (LANES, LANES), 1)
    same_head = (row // CHUNK) == (col // CHUNK)
    causal = same_head & (row >= col)
    strict = same_head & (row > col)
    blk16 = (row // 16) == (col // 16)
    eye = (row == col).astype(F32)
    lane_lo = lax.broadcasted_iota(jnp.int32, (CHUNK, LANES), 1) < HEAD_DIM

    def stack(x):
        return jnp.concatenate([jnp.where(lane_lo, x, 0.0), jnp.where(lane_lo, 0.0, x)], axis=0)

    def rep_cols(x, c1, c2):
        return jnp.concatenate([jnp.broadcast_to(x[:, c1:c1 + 1], (CHUNK, LANES)),
                                jnp.broadcast_to(x[:, c2:c2 + 1], (CHUNK, LANES))], axis=0)

    n_chunks = ts // CHUNK
    n_groups = n_chunks // CHUNKS_PER_GROUP
    group = [(ci, p) for ci in range(CHUNKS_PER_GROUP) for p in range(N_PAIRS)]

    def group_rows_idx(grp):
        rows, idx = [], []
        for ci, p in group:
            c = grp * CHUNKS_PER_GROUP + ci
            rows.append(pl.multiple_of(c * CHUNK, CHUNK))
            idx.append(c * N_PAIRS + p)
        return rows, idx

    def solve_group(grp, carry):
        rows, idx = group_rows_idx(grp)
        n = len(group)
        kk, qq, vb, kb, k_dec, eg, decay = [], [], [], [], [], [], []
        for i, (ci, p) in enumerate(group):
            sl = slice(p * LANES, (p + 1) * LANES)
            brep = rep_cols(bs[pl.ds(rows[i], CHUNK), :], 2 * p, 2 * p + 1)
            g = rep_cols(gs[pl.ds(rows[i], CHUNK), :], DN_HEADS + 2 * p, DN_HEADS + 2 * p + 1)
            gl = jnp.concatenate(
                [jnp.broadcast_to(g[CHUNK - 1:CHUNK, :], (CHUNK, LANES)),
                 jnp.broadcast_to(g[2 * CHUNK - 1:2 * CHUNK, :], (CHUNK, LANES))], axis=0)
            k = stack(ks[pl.ds(rows[i], CHUNK), sl])
            kk.append(k)
            kb.append(k * brep)
            vb.append(stack(vs[pl.ds(rows[i], CHUNK), sl]) * brep)
            qq.append(stack(qs[pl.ds(rows[i], CHUNK), sl]))
            eg.append(jnp.exp(g))
            k_dec.append(k * jnp.exp(gl - g))
            decay.append(jnp.exp(jnp.where(causal, g - g.T, NEG_BIG)))
            egl_ref[idx[i]] = jnp.exp(gl)
        k_bf = [k.astype(BF16) for k in kk]
        a_mat = [jnp.where(strict, _mm_nt(kb[i], k_bf[i]) * decay[i], 0.0) for i in range(n)]
        for i in range(n):
            qk_ref[idx[i]] = (_mm_nt(qq[i], k_bf[i]) * decay[i]).astype(BF16)
        bm = [jnp.where(blk16, -a, 0.0) for a in a_mat]
        lm = [jnp.where(blk16, 0.0, a).astype(BF16) for a in a_mat]
        dinv = [eye + b for b in bm]
        b2 = [_mm(b, b).astype(BF16) for b in bm]
        b4 = [_mm(b, b).astype(BF16) for b in b2]
        dinv = [d + _mm(d, b) for d, b in zip(dinv, b2)]
        b8 = [_mm(b, b) for b in b4]
        dinv = [d + _mm(d, b) for d, b in zip(dinv, b4)]
        dinv = [(d + _mm(d, b)).astype(BF16) for d, b in zip(dinv, b8)]
        nm = [(-_mm(d, l)).astype(BF16) for d, l in zip(dinv, lm)]
        y = [_mm(dinv[i], jnp.concatenate([vb[i], kb[i] * eg[i]], axis=1)) for i in range(n)]
        n2 = [_mm(m, m) for m in nm]
        y = [yy + _mm(m, yy) for yy, m in zip(y, n2)]
        y = [yy + _mm(m, yy) for yy, m in zip(y, nm)]
        for i in range(n):
            cp = _mm(k_dec[i].T, y[i])
            c_ref[idx[i]] = cp[:, 0:LANES]
            p_ref[idx[i]] = cp[:, LANES:2 * LANES].astype(BF16)
            u_ref[idx[i]] = y[i][:, 0:LANES]
            wq_ref[idx[i], 0:LANES, :] = y[i][:, LANES:2 * LANES].astype(BF16)
            wq_ref[idx[i], LANES:2 * LANES, :] = (qq[i] * eg[i]).astype(BF16)
        return carry

    lax.fori_loop(0, n_groups, solve_group, 0)

    for c in range(n_chunks):
        s_old = [s_ref[p] for p in range(N_PAIRS)]
        s_bf = [s.astype(BF16) for s in s_old]
        for p in range(N_PAIRS):
            sp_ref[c * N_PAIRS + p] = s_bf[p]
        upd = [_mm(p_ref[c * N_PAIRS + p], s_bf[p]) for p in range(N_PAIRS)]
        for p in range(N_PAIRS):
            i = c * N_PAIRS + p
            s_ref[p] = s_old[p] * egl_ref[i] - upd[p] + c_ref[i]

    def output_group(grp, carry):
        rows, idx = group_rows_idx(grp)
        ws_qs = [_mm(wq_ref[i], sp_ref[i]) for i in idx]
        v_new = [u_ref[i] - x[0:LANES, :] for i, x in zip(idx, ws_qs)]
        o2 = [x[LANES:2 * LANES, :] + _mm(qk_ref[i], v) for i, x, v in zip(idx, ws_qs, v_new)]
        for (ci, p), r0, o in zip(group, rows, o2):
            osc[pl.ds(r0, CHUNK), p * LANES:(p + 1) * LANES] = o[0:CHUNK, :] + o[CHUNK:2 * CHUNK, :]
        return carry

    lax.fori_loop(0, n_groups, output_group, 0)

    for p in range(N_PAIRS):
        sl = slice(p * LANES, (p + 1) * LANES)
        o = osc[:, sl]
        ms = _group_sum(o * o, ones2) * (1.0 / HEAD_DIM)
        z = proj_ref[:, Z_OFF + p * LANES:Z_OFF + (p + 1) * LANES].astype(F32)
        out_ref[:, sl] = (o * lax.rsqrt(ms + EPS) * hn_ref[:, sl] * _silu(z)).astype(out_ref.dtype)

    cs[HALO:HALO + ts, :] = (proj_ref[:, SCC_OFF:SCC_OFF + SC_WIDTH].astype(F32)
                             * proj_ref[:, SCX_OFF:SCX_OFF + SC_WIDTH].astype(F32))
    for p in range(SC_WIDTH // LANES):
        sl = slice(p * LANES, (p + 1) * LANES)
        acc = cs[HALO:HALO + ts, sl] * wsc_ref[2:3, sl]
        for j in range(1, 3):
            acc = acc + cs[HALO - j:HALO - j + ts, sl] * wsc_ref[2 - j:3 - j, sl]
        y = acc * proj_ref[:, SCB_OFF + p * LANES:SCB_OFF + (p + 1) * LANES].astype(F32)
        ms = _group_sum(y * y, ones2) * (1.0 / HEAD_DIM)
        out_ref[:, DN_WIDTH + p * LANES:DN_WIDTH + (p + 1) * LANES] = (
            y * lax.rsqrt(ms + EPS) * scn_ref[:, sl]).astype(out_ref.dtype)
    cs[0:HALO, :] = cs[ts:ts + HALO, :]


def _mixer(proj, ba, wq, wsc, hn, scn, alog, dtb, ts):
    b, t, _ = proj.shape
    n_prob = (ts // CHUNK) * N_PAIRS
    const = lambda i, j: (0, 0)
    return pl.pallas_call(
        functools.partial(_mixer_kernel, ts=ts),
        grid=(b, t // ts),
        in_specs=[
            pl.BlockSpec((None, ts, MAIN_COLS), lambda i, j: (i, j, 0)),
            pl.BlockSpec((None, ts, LANES), lambda i, j: (i, j, 0)),
            pl.BlockSpec((4, QKV_WIDTH), const),
            pl.BlockSpec((3, SC_WIDTH), const),
            pl.BlockSpec((1, DN_WIDTH), const),
            pl.BlockSpec((1, SC_WIDTH), const),
            pl.BlockSpec((1, LANES), const),
            pl.BlockSpec((1, LANES), const),
        ],
        out_specs=pl.BlockSpec((None, ts, D_MODEL), lambda i, j: (i, j, 0)),
        out_shape=jax.ShapeDtypeStruct((b, t, D_MODEL), BF16),
        scratch_shapes=[
            pltpu.VMEM((HALO + ts, QKV_WIDTH), F32),
            pltpu.VMEM((HALO + ts, SC_WIDTH), F32),
            pltpu.VMEM((ts, DN_WIDTH), F32),
            pltpu.VMEM((ts, DN_WIDTH), F32),
            pltpu.VMEM((ts, DN_WIDTH), F32),
            pltpu.VMEM((ts, LANES), F32),
            pltpu.VMEM((ts, LANES), F32),
            pltpu.VMEM((ts, DN_WIDTH), F32),
            pltpu.VMEM((N_PAIRS, LANES, LANES), F32),
            pltpu.VMEM((n_prob, LANES, LANES), F32),
            pltpu.VMEM((n_prob, LANES, LANES), F32),
            pltpu.VMEM((n_prob, LANES, LANES), F32),
            pltpu.VMEM((n_prob, LANES, LANES), BF16),
            pltpu.VMEM((n_prob, LANES, LANES), BF16),
            pltpu.VMEM((n_prob, LANES, LANES), BF16),
            pltpu.VMEM((n_prob, 2 * LANES, LANES), BF16),
        ],
        compiler_params=pltpu.CompilerParams(
            dimension_semantics=("parallel", "arbitrary"), vmem_limit_bytes=VMEM_LIMIT),
        name="mixer",
    )(proj, ba, wq, wsc, hn, scn, alog, dtb)


def _ffn_kernel(x_ref, mixed_ref, wout_ref, gain_ref, wup_ref, cw_ref, wdown_ref, fin_ref,
                out_ref, work, halo, *, tm, final):
    t = pl.program_id(1)

    @pl.when(t == 0)
    def _():
        halo[...] = jnp.zeros_like(halo)

    x1 = x_ref[...] + jnp.dot(mixed_ref[...], wout_ref[...], preferred_element_type=F32)
    hb = _rms(x1, gain_ref[...]).astype(BF16)
    acc = jnp.zeros((tm, D_MODEL), F32)
    for j in range(D_FF // FF_BLK):
        sl = slice(j * FF_BLK, (j + 1) * FF_BLK)
        val = jnp.dot(hb, wup_ref[:, sl], preferred_element_type=F32)
        gate = jnp.dot(hb, wup_ref[:, D_FF + j * FF_BLK:D_FF + (j + 1) * FF_BLK],
                       preferred_element_type=F32)
        work[0:HALO, :] = halo[:, sl]
        work[HALO:HALO + tm, :] = val
        cv = (val * cw_ref[2:3, sl] + work[HALO - 1:HALO - 1 + tm, :] * cw_ref[1:2, sl]
              + work[HALO - 2:HALO - 2 + tm, :] * cw_ref[0:1, sl])
        halo[:, sl] = work[tm:tm + HALO, :]
        act = (_silu(cv) * gate).astype(BF16)
        acc = acc + jnp.dot(act, wdown_ref[sl, :], preferred_element_type=F32)
    x2 = x1 + acc
    if final:
        x2 = _rms(x2, fin_ref[...])
    out_ref[...] = x2


def _ffn(x, mixed, w_out, gain, w_up, cw, w_down, fin, tm, final):
    b, t, _ = x.shape
    const = lambda i, j: (0, 0)
    once = pl.Buffered(1)
    return pl.pallas_call(
        functools.partial(_ffn_kernel, tm=tm, final=final),
        grid=(b, t // tm),
        in_specs=[
            pl.BlockSpec((None, tm, D_MODEL), lambda i, j: (i, j, 0)),
            pl.BlockSpec((None, tm, D_MODEL), lambda i, j: (i, j, 0)),
            pl.BlockSpec((D_MODEL, D_MODEL), const, pipeline_mode=once),
            pl.BlockSpec((1, D_MODEL), const),
            pl.BlockSpec((D_MODEL, 2 * D_FF), const, pipeline_mode=once),
            pl.BlockSpec((3, D_FF), const),
            pl.BlockSpec((D_FF, D_MODEL), const, pipeline_mode=once),
            pl.BlockSpec((1, D_MODEL), const),
        ],
        out_specs=pl.BlockSpec((None, tm, D_MODEL), lambda i, j: (i, j, 0)),
        out_shape=jax.ShapeDtypeStruct((b, t, D_MODEL), F32),
        scratch_shapes=[
            pltpu.VMEM((HALO + tm, FF_BLK), F32),
            pltpu.VMEM((HALO, D_FF), F32),
        ],
        compiler_params=pltpu.CompilerParams(
            dimension_semantics=("parallel", "arbitrary"), vmem_limit_bytes=VMEM_LIMIT),
        name="ffn",
    )(x, mixed, w_out, gain, w_up, cw, w_down, fin)


def kernel(x, attn_norm, w_in, conv_qkv, a_log, dt_bias, head_norm, conv_sc, sc_norm, w_out,
           ffn_norm, w_up, conv_ffn, w_down, final_norm):
    b, t, d = x.shape
    depth = w_in.shape[0]
    ba_lo = QKV_WIDTH + DN_WIDTH
    ba_hi = ba_lo + 2 * DN_HEADS
    fin = final_norm.reshape(1, d)
    for l in range(depth):
        w = w_in[l]
        w_main = jnp.concatenate([w[:, :ba_lo], w[:, ba_hi:]], axis=1).astype(BF16)
        w_ba = jnp.pad(w[:, ba_lo:ba_hi], ((0, 0), (0, LANES - 2 * DN_HEADS))).astype(BF16)
        proj, ba = _inproj(x.reshape(b * t, d), attn_norm[l].reshape(1, d), w_main, w_ba, tm=512)
        alog = jnp.pad(a_log[l], (DN_HEADS, LANES - 2 * DN_HEADS)).reshape(1, LANES)
        dtb = jnp.pad(dt_bias[l], (DN_HEADS, LANES - 2 * DN_HEADS)).reshape(1, LANES)
        mixed = _mixer(proj.reshape(b, t, MAIN_COLS), ba.reshape(b, t, LANES),
                       conv_qkv[l], conv_sc[l],
                       jnp.tile(head_norm[l], DN_HEADS).reshape(1, DN_WIDTH),
                       sc_norm[l].reshape(1, SC_WIDTH), alog, dtb, ts=256)
        x = _ffn(x, mixed, w_out[l].astype(BF16), ffn_norm[l].reshape(1, d),
                 w_up[l].astype(BF16), conv_ffn[l], w_down[l].astype(BF16), fin,
                 tm=512, final=(l == depth - 1))
    return x
```

```python
import functools

import jax
import jax.numpy as jnp
from jax import lax
from jax.experimental import pallas as pl
from jax.experimental.pallas import tpu as pltpu

F32 = jnp.float32
BF16 = jnp.bfloat16

D_MODEL = 1024
DN_HEADS = 8
HEAD_DIM = 64
DN_WIDTH = DN_HEADS * HEAD_DIM
SC_WIDTH = D_MODEL - DN_WIDTH
QKV_WIDTH = 3 * DN_WIDTH
D_FF = 2816
CHUNK = 64
EPS = 1e-6
LANES = 128
N_PAIRS = DN_HEADS // 2
MAIN_COLS = QKV_WIDTH + DN_WIDTH + 3 * SC_WIDTH
Z_OFF = QKV_WIDTH
SCB_OFF = Z_OFF + DN_WIDTH
SCC_OFF = SCB_OFF + SC_WIDTH
SCX_OFF = SCC_OFF + SC_WIDTH
HALO = 8
COL_BLK = 256
NEG_BIG = -1e30
CHUNKS_PER_GROUP = 4
VMEM_LIMIT = 56 * 1024 * 1024
TOKENS_PER_STEP = 512
DELTA_TOKENS_PER_STEP = 256


def _mm(a, b):
    return jnp.dot(a.astype(BF16), b.astype(BF16), preferred_element_type=F32)


def _mm_nt(a, b):
    return lax.dot_general(a.astype(BF16), b.astype(BF16), (((1,), (1,)), ((), ())),
                           preferred_element_type=F32)


def _rms(x, gain):
    return x * lax.rsqrt(jnp.mean(x * x, axis=-1, keepdims=True) + EPS) * gain


def _silu(x):
    return x * jax.nn.sigmoid(x)


def _group_ones(n):
    r = lax.broadcasted_iota(jnp.int32, (n, n), 0) // HEAD_DIM
    c = lax.broadcasted_iota(jnp.int32, (n, n), 1) // HEAD_DIM
    return (r == c).astype(BF16)


def _group_sum(x, ones):
    return jnp.dot(x.astype(BF16), ones, preferred_element_type=F32)


def _causal_conv(pre, work, halo, sl, cw_ref, width, rows):
    work[0:HALO, :] = halo[:, sl]
    work[HALO:HALO + rows, :] = pre
    acc = pre * cw_ref[width - 1:width, sl]
    for j in range(1, width):
        acc = acc + work[HALO - j:HALO - j + rows, :] * cw_ref[width - 1 - j:width - j, sl]
    halo[:, sl] = work[rows:rows + HALO, :]
    return acc


def _inproj_kernel(x_ref, gain_ref, w_ref, wba_ref, cwq_ref, cwsc_ref, scn_ref,
                   qkv_ref, zy_ref, ba_ref, work_a, work_b, halo_q, halo_s, *, tm):
    t = pl.program_id(1)

    @pl.when(t == 0)
    def _():
        halo_q[...] = jnp.zeros_like(halo_q)
        halo_s[...] = jnp.zeros_like(halo_s)

    hb = _rms(x_ref[...], gain_ref[...]).astype(BF16)
    ones = _group_ones(COL_BLK)
    works = (work_a, work_b)

    def proj(off):
        return jnp.dot(hb, w_ref[:, off:off + COL_BLK], preferred_element_type=F32)

    def qkv_epilogue(j, pre):
        sl = slice(j * COL_BLK, (j + 1) * COL_BLK)
        y = _silu(_causal_conv(pre[0], works[j % 2], halo_q, sl, cwq_ref, 4, tm))
        if j * COL_BLK < 2 * DN_WIDTH:
            y = y * lax.rsqrt(_group_sum(y * y, ones) + EPS)
            if j * COL_BLK < DN_WIDTH:
                y = y * (HEAD_DIM ** -0.5)
        qkv_ref[:, sl] = y.astype(BF16)

    def z_epilogue(j, pre):
        zy_ref[:, j * COL_BLK:(j + 1) * COL_BLK] = pre[0].astype(BF16)

    def sc_epilogue(j, pre):
        sl = slice(j * COL_BLK, (j + 1) * COL_BLK)
        y = pre[0] * _causal_conv(pre[1] * pre[2], works[j % 2], halo_s, sl, cwsc_ref, 3, tm)
        ms = _group_sum(y * y, ones) * (1.0 / HEAD_DIM)
        zy_ref[:, DN_WIDTH + j * COL_BLK:DN_WIDTH + (j + 1) * COL_BLK] = (
            y * lax.rsqrt(ms + EPS) * scn_ref[:, sl]).astype(BF16)

    def ba_epilogue(j, pre):
        ba_ref[...] = pre[0]

    tasks = []
    for j in range(QKV_WIDTH // COL_BLK):
        tasks.append((qkv_epilogue, j, (j * COL_BLK,)))
    for j in range(DN_WIDTH // COL_BLK):
        tasks.append((z_epilogue, j, (Z_OFF + j * COL_BLK,)))
    for j in range(SC_WIDTH // COL_BLK):
        tasks.append((sc_epilogue, j, tuple(o + j * COL_BLK for o in (SCB_OFF, SCC_OFF, SCX_OFF))))
    pre = tuple(proj(o) for o in tasks[0][2])
    for i, (epilogue, j, _) in enumerate(tasks):
        if i + 1 < len(tasks):
            nxt = tuple(proj(o) for o in tasks[i + 1][2])
        else:
            nxt = (jnp.dot(hb, wba_ref[...], preferred_element_type=F32),)
        epilogue(j, pre)
        pre = nxt
    ba_epilogue(0, pre)


def _inproj(x, gain, w_main, w_ba, cwq, cwsc, scn, tm):
    b, t, _ = x.shape
    const = lambda i, j: (0, 0)
    once = pl.Buffered(1)
    tok = lambda i, j: (i, j, 0)
    return pl.pallas_call(
        functools.partial(_inproj_kernel, tm=tm),
        grid=(b, t // tm),
        in_specs=[
            pl.BlockSpec((None, tm, D_MODEL), tok),
            pl.BlockSpec((1, D_MODEL), const),
            pl.BlockSpec((D_MODEL, MAIN_COLS), const, pipeline_mode=once),
            pl.BlockSpec((D_MODEL, LANES), const, pipeline_mode=once),
            pl.BlockSpec((4, QKV_WIDTH), const),
            pl.BlockSpec((3, SC_WIDTH), const),
            pl.BlockSpec((1, SC_WIDTH), const),
        ],
        out_specs=[
            pl.BlockSpec((None, tm, QKV_WIDTH), tok),
            pl.BlockSpec((None, tm, D_MODEL), tok),
            pl.BlockSpec((None, tm, LANES), tok),
        ],
        out_shape=[
            jax.ShapeDtypeStruct((b, t, QKV_WIDTH), BF16),
            jax.ShapeDtypeStruct((b, t, D_MODEL), BF16),
            jax.ShapeDtypeStruct((b, t, LANES), F32),
        ],
        scratch_shapes=[
            pltpu.VMEM((HALO + tm, COL_BLK), F32),
            pltpu.VMEM((HALO + tm, COL_BLK), F32),
            pltpu.VMEM((HALO, QKV_WIDTH), F32),
            pltpu.VMEM((HALO, SC_WIDTH), F32),
        ],
        compiler_params=pltpu.CompilerParams(
            dimension_semantics=("parallel", "arbitrary"), vmem_limit_bytes=VMEM_LIMIT),
        name="inproj",
    )(x, gain, w_main, w_ba, cwq, cwsc, scn)


def _delta_kernel(qkv_ref, ba_ref, alog_ref, dtb_ref, o_ref, bs, gs, s_ref,
                  egl_ref, c_ref, u_ref, p_ref, qk_ref, sp_ref, wq_ref, *, ts):
    t = pl.program_id(1)

    @pl.when(t == 0)
    def _():
        s_ref[...] = jnp.zeros_like(s_ref)

    ba = ba_ref[...]
    bs[...] = jax.nn.sigmoid(ba)
    sp_in = ba + dtb_ref[...]
    softplus = jnp.maximum(sp_in, 0.0) + jnp.log1p(jnp.exp(-jnp.abs(sp_in)))
    g_raw = -jnp.exp(alog_ref[...]) * softplus
    rt = lax.broadcasted_iota(jnp.int32, (ts, ts), 0)
    ct = lax.broadcasted_iota(jnp.int32, (ts, ts), 1)
    tri = ((rt >= ct) & (rt // CHUNK == ct // CHUNK)).astype(BF16)
    g_hi = g_raw.astype(BF16)
    g_r1 = g_raw - g_hi.astype(F32)
    g_mid = g_r1.astype(BF16)
    g_lo = (g_r1 - g_mid.astype(F32)).astype(BF16)
    gs[...] = (jnp.dot(tri, g_hi, preferred_element_type=F32)
               + jnp.dot(tri, g_mid, preferred_element_type=F32)
               + jnp.dot(tri, g_lo, preferred_element_type=F32))

    row = lax.broadcasted_iota(jnp.int32, (LANES, LANES), 0)
    col = lax.broadcasted_iota(jnp.int32, (LANES, LANES), 1)
    same_head = (row // CHUNK) == (col // CHUNK)
    causal = same_head & (row >= col)
    strict = same_head & (row > col)
    blk16 = (row // 16) == (col // 16)
    eye = (row == col).astype(F32)
    lane_lo = lax.broadcasted_iota(jnp.int32, (CHUNK, LANES), 1) < HEAD_DIM

    def stack(x):
        zero = jnp.zeros_like(x)
        return jnp.concatenate([jnp.where(lane_lo, x, zero), jnp.where(lane_lo, zero, x)], axis=0)

    def rep_cols(x, c1, c2):
        return jnp.concatenate([jnp.broadcast_to(x[:, c1:c1 + 1], (CHUNK, LANES)),
                                jnp.broadcast_to(x[:, c2:c2 + 1], (CHUNK, LANES))], axis=0)

    n_chunks = ts // CHUNK
    n_groups = n_chunks // CHUNKS_PER_GROUP
    group = [(ci, p) for ci in range(CHUNKS_PER_GROUP) for p in range(N_PAIRS)]

    def group_rows_idx(grp):
        rows, idx = [], []
        for ci, p in group:
            c = grp * CHUNKS_PER_GROUP + ci
            rows.append(pl.multiple_of(c * CHUNK, CHUNK))
            idx.append(c * N_PAIRS + p)
        return rows, idx

    def solve_group(grp, carry):
        rows, idx = group_rows_idx(grp)
        n = len(group)
        k_bf, q_bf, kk, qq, vb, kb, k_dec, eg, decay = [], [], [], [], [], [], [], [], []
        for i, (ci, p) in enumerate(group):
            rs = pl.ds(rows[i], CHUNK)
            brep = rep_cols(bs[rs, :], 2 * p, 2 * p + 1)
            g = rep_cols(gs[rs, :], DN_HEADS + 2 * p, DN_HEADS + 2 * p + 1)
            gl = jnp.concatenate(
                [jnp.broadcast_to(g[CHUNK - 1:CHUNK, :], (CHUNK, LANES)),
                 jnp.broadcast_to(g[2 * CHUNK - 1:2 * CHUNK, :], (CHUNK, LANES))], axis=0)
            q_bf.append(stack(qkv_ref[rs, p * LANES:(p + 1) * LANES]))
            k_bf.append(stack(qkv_ref[rs, DN_WIDTH + p * LANES:DN_WIDTH + (p + 1) * LANES]))
            v = stack(qkv_ref[rs, 2 * DN_WIDTH + p * LANES:2 * DN_WIDTH + (p + 1) * LANES])
            k = k_bf[i].astype(F32)
            kb.append(k * brep)
            vb.append(v.astype(F32) * brep)
            qq.append(q_bf[i].astype(F32))
            eg.append(jnp.exp(g))
            k_dec.append(k * jnp.exp(gl - g))
            decay.append(jnp.exp(jnp.where(causal, g - g.T, NEG_BIG)))
            egl_ref[idx[i]] = jnp.exp(gl)
        a_mat = [jnp.where(strict, _mm_nt(kb[i], k_bf[i]) * decay[i], 0.0) for i in range(n)]
        for i in range(n):
            qk_ref[idx[i]] = (_mm_nt(q_bf[i], k_bf[i]) * decay[i]).astype(BF16)
        bm = [jnp.where(blk16, -a, 0.0) for a in a_mat]
        lm = [jnp.where(blk16, 0.0, a).astype(BF16) for a in a_mat]
        dinv = [eye + b for b in bm]
        b2 = [_mm(b, b).astype(BF16) for b in bm]
        b4 = [_mm(b, b).astype(BF16) for b in b2]
        dinv = [d + _mm(d, b) for d, b in zip(dinv, b2)]
        b8 = [_mm(b, b) for b in b4]
        dinv = [d + _mm(d, b) for d, b in zip(dinv, b4)]
        dinv = [d + _mm(d, b) for d, b in zip(dinv, b8)]
        dinv_bf = [d.astype(BF16) for d in dinv]
        nm = [(-_mm(d, l)).astype(BF16) for d, l in zip(dinv_bf, lm)]
        n2 = [_mm(m, m) for m in nm]
        tinv = [d + _mm(m, db) for d, db, m in zip(dinv, dinv_bf, n2)]
        tinv = [tt + _mm(m, tt) for tt, m in zip(tinv, nm)]
        y = [_mm(tinv[i], jnp.concatenate([vb[i], kb[i] * eg[i]], axis=1)) for i in range(n)]
        for i in range(n):
            cp = _mm(k_dec[i].T, y[i])
            c_ref[idx[i]] = cp[:, 0:LANES]
            p_ref[idx[i]] = cp[:, LANES:2 * LANES].astype(BF16)
            u_ref[idx[i]] = y[i][:, 0:LANES]
            wq_ref[idx[i], 0:LANES, :] = y[i][:, LANES:2 * LANES].astype(BF16)
            wq_ref[idx[i], LANES:2 * LANES, :] = (qq[i] * eg[i]).astype(BF16)
        return carry

    lax.fori_loop(0, n_groups, solve_group, 0)

    for c in range(n_chunks):
        s_old = [s_ref[p] for p in range(N_PAIRS)]
        s_bf = [s.astype(BF16) for s in s_old]
        for p in range(N_PAIRS):
            sp_ref[c * N_PAIRS + p] = s_bf[p]
        upd = [_mm(p_ref[c * N_PAIRS + p], s_bf[p]) for p in range(N_PAIRS)]
        for p in range(N_PAIRS):
            i = c * N_PAIRS + p
            s_ref[p] = s_old[p] * egl_ref[i] - upd[p] + c_ref[i]

    def output_group(grp, carry):
        rows, idx = group_rows_idx(grp)
        ws_qs = [_mm(wq_ref[i], sp_ref[i]) for i in idx]
        v_new = [u_ref[i] - x[0:LANES, :] for i, x in zip(idx, ws_qs)]
        o2 = [x[LANES:2 * LANES, :] + _mm(qk_ref[i], v) for i, x, v in zip(idx, ws_qs, v_new)]
        for (ci, p), r0, o in zip(group, rows, o2):
            o_ref[pl.ds(r0, CHUNK), p * LANES:(p + 1) * LANES] = o[0:CHUNK, :] + o[CHUNK:2 * CHUNK, :]
        return carry

    lax.fori_loop(0, n_groups, output_group, 0)


def _delta(qkv, ba, alog, dtb, ts):
    b, t, _ = qkv.shape
    n_prob = (ts // CHUNK) * N_PAIRS
    const = lambda i, j: (0, 0)
    tok = lambda i, j: (i, j, 0)
    return pl.pallas_call(
        functools.partial(_delta_kernel, ts=ts),
        grid=(b, t // ts),
        in_specs=[
            pl.BlockSpec((None, ts, QKV_WIDTH), tok),
            pl.BlockSpec((None, ts, LANES), tok),
            pl.BlockSpec((1, LANES), const),
            pl.BlockSpec((1, LANES), const),
        ],
        out_specs=pl.BlockSpec((None, ts, DN_WIDTH), tok),
        out_shape=jax.ShapeDtypeStruct((b, t, DN_WIDTH), F32),
        scratch_shapes=[
            pltpu.VMEM((ts, LANES), F32),
            pltpu.VMEM((ts, LANES), F32),
            pltpu.VMEM((N_PAIRS, LANES, LANES), F32),
            pltpu.VMEM((n_prob, LANES, LANES), F32),
            pltpu.VMEM((n_prob, LANES, LANES), F32),
            pltpu.VMEM((n_prob, LANES, LANES), F32),
            pltpu.VMEM((n_prob, LANES, LANES), BF16),
            pltpu.VMEM((n_prob, LANES, LANES), BF16),
            pltpu.VMEM((n_prob, LANES, LANES), BF16),
            pltpu.VMEM((n_prob, 2 * LANES, LANES), BF16),
        ],
        compiler_params=pltpu.CompilerParams(
            dimension_semantics=("parallel", "arbitrary"), vmem_limit_bytes=VMEM_LIMIT),
        name="delta",
    )(qkv, ba, alog, dtb)


def _ffn_kernel(x_ref, o_ref, zy_ref, hn_ref, wout_ref, gain_ref, wup_ref, cw_ref, wdown_ref,
                fin_ref, out_ref, work_a, work_b, halo, *, tm, final):
    t = pl.program_id(1)
    works = (work_a, work_b)

    @pl.when(t == 0)
    def _():
        halo[...] = jnp.zeros_like(halo)

    ones = _group_ones(COL_BLK)
    x1 = x_ref[...] + jnp.dot(zy_ref[:, DN_WIDTH:D_MODEL], wout_ref[DN_WIDTH:D_MODEL, :],
                              preferred_element_type=F32)
    for j in range(DN_WIDTH // COL_BLK):
        sl = slice(j * COL_BLK, (j + 1) * COL_BLK)
        o = o_ref[:, sl]
        ms = _group_sum(o * o, ones) * (1.0 / HEAD_DIM)
        o_dn = o * lax.rsqrt(ms + EPS) * hn_ref[:, sl] * _silu(zy_ref[:, sl].astype(F32))
        x1 = x1 + jnp.dot(o_dn.astype(BF16), wout_ref[sl, :], preferred_element_type=F32)
    hb = _rms(x1, gain_ref[...]).astype(BF16)
    def up(j):
        return (jnp.dot(hb, wup_ref[:, j * COL_BLK:(j + 1) * COL_BLK], preferred_element_type=F32),
                jnp.dot(hb, wup_ref[:, D_FF + j * COL_BLK:D_FF + (j + 1) * COL_BLK],
                        preferred_element_type=F32))

    n_blk = D_FF // COL_BLK
    acc = jnp.zeros((tm, D_MODEL), F32)
    val, gate = up(0)
    for j in range(n_blk):
        nxt = up(j + 1) if j + 1 < n_blk else None
        sl = slice(j * COL_BLK, (j + 1) * COL_BLK)
        cv = _causal_conv(val, works[j % 2], halo, sl, cw_ref, 3, tm)
        act = (_silu(cv) * gate).astype(BF16)
        acc = acc + jnp.dot(act, wdown_ref[sl, :], preferred_element_type=F32)
        if nxt is not None:
            val, gate = nxt
    x2 = x1 + acc
    if final:
        x2 = _rms(x2, fin_ref[...])
    out_ref[...] = x2


def _ffn(x, o, zy, hn, w_out, gain, w_up, cw, w_down, fin, tm, final):
    b, t, _ = x.shape
    const = lambda i, j: (0, 0)
    tok = lambda i, j: (i, j, 0)
    once = pl.Buffered(1)
    return pl.pallas_call(
        functools.partial(_ffn_kernel, tm=tm, final=final),
        grid=(b, t // tm),
        in_specs=[
            pl.BlockSpec((None, tm, D_MODEL), tok),
            pl.BlockSpec((None, tm, DN_WIDTH), tok),
            pl.BlockSpec((None, tm, D_MODEL), tok),
            pl.BlockSpec((1, DN_WIDTH), const),
            pl.BlockSpec((D_MODEL, D_MODEL), const, pipeline_mode=once),
            pl.BlockSpec((1, D_MODEL), const),
            pl.BlockSpec((D_MODEL, 2 * D_FF), const, pipeline_mode=once),
            pl.BlockSpec((3, D_FF), const),
            pl.BlockSpec((D_FF, D_MODEL), const, pipeline_mode=once),
            pl.BlockSpec((1, D_MODEL), const),
        ],
        out_specs=pl.BlockSpec((None, tm, D_MODEL), tok),
        out_shape=jax.ShapeDtypeStruct((b, t, D_MODEL), F32),
        scratch_shapes=[
            pltpu.VMEM((HALO + tm, COL_BLK), F32),
            pltpu.VMEM((HALO + tm, COL_BLK), F32),
            pltpu.VMEM((HALO, D_FF), F32),
        ],
        compiler_params=pltpu.CompilerParams(
            dimension_semantics=("parallel", "arbitrary"), vmem_limit_bytes=VMEM_LIMIT),
        name="ffn",
    )(x, o, zy, hn, w_out, gain, w_up, cw, w_down, fin)


def kernel(x, attn_norm, w_in, conv_qkv, a_log, dt_bias, head_norm, conv_sc, sc_norm, w_out,
           ffn_norm, w_up, conv_ffn, w_down, final_norm):
    d = x.shape[-1]
    depth = w_in.shape[0]
    ba_lo = QKV_WIDTH + DN_WIDTH
    ba_hi = ba_lo + 2 * DN_HEADS
    fin = final_norm.reshape(1, d)
    for l in range(depth):
        w = w_in[l]
        w_main = jnp.concatenate([w[:, :ba_lo], w[:, ba_hi:]], axis=1).astype(BF16)
        w_ba = jnp.pad(w[:, ba_lo:ba_hi], ((0, 0), (0, LANES - 2 * DN_HEADS))).astype(BF16)
        qkv, zy, ba = _inproj(x, attn_norm[l].reshape(1, d), w_main, w_ba, conv_qkv[l], conv_sc[l],
                              sc_norm[l].reshape(1, SC_WIDTH), tm=TOKENS_PER_STEP)
        alog = jnp.pad(a_log[l], (DN_HEADS, LANES - 2 * DN_HEADS)).reshape(1, LANES)
        dtb = jnp.pad(dt_bias[l], (DN_HEADS, LANES - 2 * DN_HEADS)).reshape(1, LANES)
        o = _delta(qkv, ba, alog, dtb, ts=DELTA_TOKENS_PER_STEP)
        x = _ffn(x, o, zy, jnp.tile(head_norm[l], DN_HEADS).reshape(1, DN_WIDTH),
                 w_out[l].astype(BF16), ffn_norm[l].reshape(1, d), w_up[l].astype(BF16),
                 conv_ffn[l], w_down[l].astype(BF16), fin, tm=TOKENS_PER_STEP,
                 final=(l == depth - 1))
    return x
```

```python
import functools

import jax
import jax.numpy as jnp
from jax import lax
from jax.experimental import pallas as pl
from jax.experimental.pallas import tpu as pltpu

F32 = jnp.float32
BF16 = jnp.bfloat16

D_MODEL = 1024
DN_HEADS = 8
HEAD_DIM = 64
DN_WIDTH = DN_HEADS * HEAD_DIM
SC_WIDTH = D_MODEL - DN_WIDTH
QKV_WIDTH = 3 * DN_WIDTH
D_FF = 2816
CHUNK = 64
EPS = 1e-6
LANES = 128
N_PAIRS = DN_HEADS // 2
MAIN_COLS = QKV_WIDTH + DN_WIDTH + 3 * SC_WIDTH
Z_OFF = QKV_WIDTH
SCB_OFF = Z_OFF + DN_WIDTH
SCC_OFF = SCB_OFF + SC_WIDTH
SCX_OFF = SCC_OFF + SC_WIDTH
HALO = 8
COL_BLK = 256
NEG_BIG = -1e30

VMEM_LIMIT = 56 * 1024 * 1024
TOKENS_PER_STEP = 512
DELTA_TOKENS_PER_STEP = 1024
SUB_TOKENS = 256


def _mm(a, b):
    return jnp.dot(a.astype(BF16), b.astype(BF16), preferred_element_type=F32)


def _mm_nt(a, b):
    return lax.dot_general(a.astype(BF16), b.astype(BF16), (((1,), (1,)), ((), ())),
                           preferred_element_type=F32)


def _rms(x, gain):
    return x * lax.rsqrt(jnp.mean(x * x, axis=-1, keepdims=True) + EPS) * gain


def _silu(x):
    return x * jax.nn.sigmoid(x)


def _group_ones(n):
    r = lax.broadcasted_iota(jnp.int32, (n, n), 0) // HEAD_DIM
    c = lax.broadcasted_iota(jnp.int32, (n, n), 1) // HEAD_DIM
    return (r == c).astype(BF16)


def _group_sum(x, ones):
    return jnp.dot(x.astype(BF16), ones, preferred_element_type=F32)


def _causal_conv(pre, work, halo, sl, cw_ref, width, rows):
    work[0:HALO, :] = halo[:, sl]
    work[HALO:HALO + rows, :] = pre
    acc = pre * cw_ref[width - 1:width, sl]
    for j in range(1, width):
        acc = acc + work[HALO - j:HALO - j + rows, :] * cw_ref[width - 1 - j:width - j, sl]
    halo[:, sl] = work[rows:rows + HALO, :]
    return acc


def _inproj_kernel(x_ref, gain_ref, w_ref, wba_ref, cwq_ref, cwsc_ref, scn_ref,
                   qkv_ref, zy_ref, ba_ref, work_a, work_b, halo_q, halo_s, *, tm):
    t = pl.program_id(1)

    @pl.when(t == 0)
    def _():
        halo_q[...] = jnp.zeros_like(halo_q)
        halo_s[...] = jnp.zeros_like(halo_s)

    hb = _rms(x_ref[...], gain_ref[...]).astype(BF16)
    ones = _group_ones(COL_BLK)
    works = (work_a, work_b)

    def proj(off):
        return jnp.dot(hb, w_ref[:, off:off + COL_BLK], preferred_element_type=F32)

    def qkv_epilogue(j, pre):
        sl = slice(j * COL_BLK, (j + 1) * COL_BLK)
        y = _silu(_causal_conv(pre[0], works[j % 2], halo_q, sl, cwq_ref, 4, tm))
        if j * COL_BLK < 2 * DN_WIDTH:
            y = y * lax.rsqrt(_group_sum(y * y, ones) + EPS)
            if j * COL_BLK < DN_WIDTH:
                y = y * (HEAD_DIM ** -0.5)
        qkv_ref[:, sl] = y.astype(BF16)

    def z_epilogue(j, pre):
        zy_ref[:, j * COL_BLK:(j + 1) * COL_BLK] = pre[0].astype(BF16)

    def sc_epilogue(j, pre):
        sl = slice(j * COL_BLK, (j + 1) * COL_BLK)
        y = pre[0] * _causal_conv(pre[1] * pre[2], works[j % 2], halo_s, sl, cwsc_ref, 3, tm)
        ms = _group_sum(y * y, ones) * (1.0 / HEAD_DIM)
        zy_ref[:, DN_WIDTH + j * COL_BLK:DN_WIDTH + (j + 1) * COL_BLK] = (
            y * lax.rsqrt(ms + EPS) * scn_ref[:, sl]).astype(BF16)

    def ba_epilogue(j, pre):
        ba_ref[...] = pre[0]

    pre = {}

    def p(key, off):
        return lambda: pre.__setitem__(key, proj(off))

    def p_ba():
        pre["ba"] = jnp.dot(hb, wba_ref[...], preferred_element_type=F32)

    def e(fn, j, *keys):
        return lambda: fn(j, tuple(pre.pop(k) for k in keys))

    qkv = ["q0", "q1", "k0", "k1", "v0", "v1"]
    p_qkv = {name: p(name, j * COL_BLK) for j, name in enumerate(qkv)}
    e_qkv = {name: e(qkv_epilogue, j, name) for j, name in enumerate(qkv)}
    p_z = [p("z%d" % j, Z_OFF + j * COL_BLK) for j in range(2)]
    e_z = [e(z_epilogue, j, "z%d" % j) for j in range(2)]
    p_sc = [[p("s%s%d" % (n, j), off + j * COL_BLK)
             for n, off in (("b", SCB_OFF), ("c", SCC_OFF), ("x", SCX_OFF))] for j in range(2)]
    e_sc = [e(sc_epilogue, j, "sb%d" % j, "sc%d" % j, "sx%d" % j) for j in range(2)]
    schedule = [
        p_qkv["q0"],
        p_qkv["q1"], p_sc[0][0], e_qkv["q0"],
        p_qkv["k0"], p_sc[0][1], e_qkv["q1"],
        p_qkv["k1"], p_sc[0][2], e_qkv["k0"],
        p_qkv["v0"], p_z[0], e_qkv["k1"],
        p_qkv["v1"], p_z[1], e_sc[0], e_qkv["v0"], e_z[0],
        p_sc[1][0], p_sc[1][1], e_qkv["v1"], e_z[1],
        p_sc[1][2], p_ba,
        e_sc[1], e(ba_epilogue, 0, "ba"),
    ]
    for emit in schedule:
        emit()


def _inproj(x, layer, gain, w_main, w_ba, cwq, cwsc, scn, tm):
    b, t, _ = x.shape
    lay = lambda i, j: (layer, 0, 0)
    once = pl.Buffered(1)
    tok = lambda i, j: (i, j, 0)
    return pl.pallas_call(
        functools.partial(_inproj_kernel, tm=tm),
        grid=(b, t // tm),
        in_specs=[
            pl.BlockSpec((None, tm, D_MODEL), tok),
            pl.BlockSpec((None, 1, D_MODEL), lay),
            pl.BlockSpec((None, D_MODEL, MAIN_COLS), lay, pipeline_mode=once),
            pl.BlockSpec((None, D_MODEL, LANES), lay, pipeline_mode=once),
            pl.BlockSpec((None, 4, QKV_WIDTH), lay),
            pl.BlockSpec((None, 3, SC_WIDTH), lay),
            pl.BlockSpec((None, 1, SC_WIDTH), lay),
        ],
        out_specs=[
            pl.BlockSpec((None, tm, QKV_WIDTH), tok),
            pl.BlockSpec((None, tm, D_MODEL), tok),
            pl.BlockSpec((None, tm, LANES), tok),
        ],
        out_shape=[
            jax.ShapeDtypeStruct((b, t, QKV_WIDTH), BF16),
            jax.ShapeDtypeStruct((b, t, D_MODEL), BF16),
            jax.ShapeDtypeStruct((b, t, LANES), F32),
        ],
        scratch_shapes=[
            pltpu.VMEM((HALO + tm, COL_BLK), F32),
            pltpu.VMEM((HALO + tm, COL_BLK), F32),
            pltpu.VMEM((HALO, QKV_WIDTH), F32),
            pltpu.VMEM((HALO, SC_WIDTH), F32),
        ],
        compiler_params=pltpu.CompilerParams(
            dimension_semantics=("parallel", "arbitrary"), vmem_limit_bytes=VMEM_LIMIT),
        name="inproj",
    )(x, gain, w_main, w_ba, cwq, cwsc, scn)


def _delta_kernel(qkv_ref, ba_ref, alog_ref, dtb_ref, o_ref, bs, gs, s_ref,
                  egl_ref, c_ref, u_ref, p_ref, qk_ref, sp_ref, kdt_ref, wq_ref, rhs_ref, *, ts):
    t = pl.program_id(1)

    @pl.when(t == 0)
    def _():
        s_ref[...] = jnp.zeros_like(s_ref)

    ba = ba_ref[...]
    bs[...] = jax.nn.sigmoid(ba)
    sp_in = ba + dtb_ref[...]
    softplus = jnp.maximum(sp_in, 0.0) + jnp.log1p(jnp.exp(-jnp.abs(sp_in)))
    g_raw = -jnp.exp(alog_ref[...]) * softplus
    rt = lax.broadcasted_iota(jnp.int32, (SUB_TOKENS, SUB_TOKENS), 0)
    ct = lax.broadcasted_iota(jnp.int32, (SUB_TOKENS, SUB_TOKENS), 1)
    tri = ((rt >= ct) & (rt // CHUNK == ct // CHUNK)).astype(BF16)
    g_hi = g_raw.astype(BF16)
    g_r1 = g_raw - g_hi.astype(F32)
    g_mid = g_r1.astype(BF16)
    g_lo = (g_r1 - g_mid.astype(F32)).astype(BF16)
    for h in range(ts // SUB_TOKENS):
        rs = slice(h * SUB_TOKENS, (h + 1) * SUB_TOKENS)
        gs[rs, :] = (jnp.dot(tri, g_hi[rs, :], preferred_element_type=F32)
                     + jnp.dot(tri, g_mid[rs, :], preferred_element_type=F32)
                     + jnp.dot(tri, g_lo[rs, :], preferred_element_type=F32))

    row = lax.broadcasted_iota(jnp.int32, (LANES, LANES), 0)
    col = lax.broadcasted_iota(jnp.int32, (LANES, LANES), 1)
    same_head = (row // CHUNK) == (col // CHUNK)
    causal = same_head & (row >= col)
    strict = same_head & (row > col)
    blk16 = (row // 16) == (col // 16)
    eye = (row == col).astype(F32)
    lane_lo = lax.broadcasted_iota(jnp.int32, (CHUNK, LANES), 1) < HEAD_DIM

    def stack(x):
        zero = jnp.zeros_like(x)
        return jnp.concatenate([jnp.where(lane_lo, x, zero), jnp.where(lane_lo, zero, x)], axis=0)

    def rep_cols(x, c1, c2):
        return jnp.concatenate([jnp.broadcast_to(x[:, c1:c1 + 1], (CHUNK, LANES)),
                                jnp.broadcast_to(x[:, c2:c2 + 1], (CHUNK, LANES))], axis=0)

    n_sub = ts // SUB_TOKENS
    chunks_per_sub = SUB_TOKENS // CHUNK

    def problems(h):
        return [(h * chunks_per_sub + ci, p) for ci in range(chunks_per_sub) for p in range(N_PAIRS)]

    def prep_thunks(h, env):
        def one(c, p):
            i = c * N_PAIRS + p
            rs = slice(c * CHUNK, (c + 1) * CHUNK)
            brep = rep_cols(bs[rs, :], 2 * p, 2 * p + 1)
            g = rep_cols(gs[rs, :], DN_HEADS + 2 * p, DN_HEADS + 2 * p + 1)
            gl = jnp.concatenate(
                [jnp.broadcast_to(g[CHUNK - 1:CHUNK, :], (CHUNK, LANES)),
                 jnp.broadcast_to(g[2 * CHUNK - 1:2 * CHUNK, :], (CHUNK, LANES))], axis=0)
            q_bf = stack(qkv_ref[rs, p * LANES:(p + 1) * LANES])
            k_bf = stack(qkv_ref[rs, DN_WIDTH + p * LANES:DN_WIDTH + (p + 1) * LANES])
            v = stack(qkv_ref[rs, 2 * DN_WIDTH + p * LANES:2 * DN_WIDTH + (p + 1) * LANES])
            k = k_bf.astype(F32)
            kb = k * brep
            eg = jnp.exp(g)
            decay = jnp.exp(jnp.where(causal, g - g.T, NEG_BIG))
            egl_ref[i] = jnp.exp(gl)
            a_mat = jnp.where(strict, _mm_nt(kb, k_bf) * decay, 0.0)
            qk_ref[i] = (_mm_nt(q_bf, k_bf) * decay).astype(BF16)
            wq_ref[i, LANES:2 * LANES, :] = (q_bf.astype(F32) * eg).astype(BF16)
            kdt_ref[i] = (k * jnp.exp(gl - g)).T.astype(BF16)
            rhs_ref[i] = jnp.concatenate([v.astype(F32) * brep, kb * eg], axis=1).astype(BF16)
            env[("bm", i)] = jnp.where(blk16, -a_mat, 0.0)
            env[("lm", i)] = jnp.where(blk16, 0.0, a_mat).astype(BF16)
        return [functools.partial(one, c, p) for c, p in problems(h)]

    def chain_thunks(h, env):
        by_chunk = [[c * N_PAIRS + p for p in range(N_PAIRS)]
                    for c in range(h * chunks_per_sub, (h + 1) * chunks_per_sub)]

        def stage(fn):
            return [functools.partial(lambda ids: [fn(i) for i in ids], ids) for ids in by_chunk]

        def s_b2(i):
            env[("dinv", i)] = eye + env[("bm", i)]
            env[("b2", i)] = _mm(env[("bm", i)], env.pop(("bm", i))).astype(BF16)

        def s_b4(i):
            env[("b4", i)] = _mm(env[("b2", i)], env[("b2", i)]).astype(BF16)
            env[("dinv", i)] = env[("dinv", i)] + _mm(env[("dinv", i)], env.pop(("b2", i)))

        def s_b8(i):
            env[("b8", i)] = _mm(env[("b4", i)], env[("b4", i)])
            env[("dinv", i)] = env[("dinv", i)] + _mm(env[("dinv", i)], env.pop(("b4", i)))

        def s_dinv(i):
            env[("dinv", i)] = env[("dinv", i)] + _mm(env[("dinv", i)], env.pop(("b8", i)))
            env[("dinv_bf", i)] = env[("dinv", i)].astype(BF16)

        def s_nm(i):
            env[("nm", i)] = (-_mm(env[("dinv_bf", i)], env.pop(("lm", i)))).astype(BF16)

        def s_n2(i):
            env[("n2", i)] = _mm(env[("nm", i)], env[("nm", i)])

        def s_t1(i):
            env[("tinv", i)] = env.pop(("dinv", i)) + _mm(env.pop(("n2", i)), env.pop(("dinv_bf", i)))

        def s_t2(i):
            env[("tinv", i)] = env[("tinv", i)] + _mm(env.pop(("nm", i)), env[("tinv", i)])

        def s_y(i):
            env[("y", i)] = _mm(env.pop(("tinv", i)), rhs_ref[i])

        def s_cp(i):
            y = env.pop(("y", i))
            cp = _mm(kdt_ref[i], y)
            c_ref[i] = cp[:, 0:LANES]
            p_ref[i] = cp[:, LANES:2 * LANES].astype(BF16)
            u_ref[i] = y[:, 0:LANES]
            wq_ref[i, 0:LANES, :] = y[:, LANES:2 * LANES].astype(BF16)

        out = []
        for fn in (s_b2, s_b4, s_b8, s_dinv, s_nm, s_n2, s_t1, s_t2, s_y, s_cp):
            out.extend(stage(fn))
        return out

    def tail_thunks(h, env):
        def serial(c):
            state = env["state"]
            s_bf = [s.astype(BF16) for s in state]
            for p in range(N_PAIRS):
                sp_ref[c * N_PAIRS + p] = s_bf[p]
            upd = [_mm(p_ref[c * N_PAIRS + p], s_bf[p]) for p in range(N_PAIRS)]
            env["state"] = [state[p] * egl_ref[c * N_PAIRS + p] - upd[p] + c_ref[c * N_PAIRS + p]
                            for p in range(N_PAIRS)]

        def output(c):
            idx = [c * N_PAIRS + p for p in range(N_PAIRS)]
            ws_qs = [_mm(wq_ref[i], sp_ref[i]) for i in idx]
            v_new = [u_ref[i] - x[0:LANES, :] for i, x in zip(idx, ws_qs)]
            o2 = [x[LANES:2 * LANES, :] + _mm(qk_ref[i], v) for i, x, v in zip(idx, ws_qs, v_new)]
            for p, o in enumerate(o2):
                o_ref[c * CHUNK:(c + 1) * CHUNK, p * LANES:(p + 1) * LANES] = (
                    o[0:CHUNK, :] + o[CHUNK:2 * CHUNK, :])

        chunks = range(h * chunks_per_sub, (h + 1) * chunks_per_sub)
        return ([functools.partial(serial, c) for c in chunks]
                + [functools.partial(output, c) for c in chunks])

    def interleave(*lists):
        keyed = []
        for li, thunks in enumerate(lists):
            for k, f in enumerate(thunks):
                keyed.append(((k + 0.5) / len(thunks), li, k, f))
        keyed.sort(key=lambda t: t[:3])
        return [t[3] for t in keyed]

    env = {"state": [s_ref[p] for p in range(N_PAIRS)]}
    for step in range(n_sub + 2):
        lists = []
        if 0 <= step - 1 < n_sub:
            lists.append(chain_thunks(step - 1, env))
        if step < n_sub:
            lists.append(prep_thunks(step, env))
        if 0 <= step - 2 < n_sub:
            lists.append(tail_thunks(step - 2, env))
        for thunk in interleave(*lists):
            thunk()
    for p in range(N_PAIRS):
        s_ref[p] = env["state"][p]


def _delta(qkv, ba, layer, alog, dtb, ts):
    b, t, _ = qkv.shape
    n_prob = (ts // CHUNK) * N_PAIRS
    lay = lambda i, j: (layer, 0, 0)
    tok = lambda i, j: (i, j, 0)
    return pl.pallas_call(
        functools.partial(_delta_kernel, ts=ts),
        grid=(b, t // ts),
        in_specs=[
            pl.BlockSpec((None, ts, QKV_WIDTH), tok),
            pl.BlockSpec((None, ts, LANES), tok),
            pl.BlockSpec((None, 1, LANES), lay),
            pl.BlockSpec((None, 1, LANES), lay),
        ],
        out_specs=pl.BlockSpec((None, ts, DN_WIDTH), tok),
        out_shape=jax.ShapeDtypeStruct((b, t, DN_WIDTH), F32),
        scratch_shapes=[
            pltpu.VMEM((ts, LANES), F32),
            pltpu.VMEM((ts, LANES), F32),
            pltpu.VMEM((N_PAIRS, LANES, LANES), F32),
            pltpu.VMEM((n_prob, LANES, LANES), F32),
            pltpu.VMEM((n_prob, LANES, LANES), F32),
            pltpu.VMEM((n_prob, LANES, LANES), F32),
            pltpu.VMEM((n_prob, LANES, LANES), BF16),
            pltpu.VMEM((n_prob, LANES, LANES), BF16),
            pltpu.VMEM((n_prob, LANES, LANES), BF16),
            pltpu.VMEM((n_prob, LANES, LANES), BF16),
            pltpu.VMEM((n_prob, 2 * LANES, LANES), BF16),
            pltpu.VMEM((n_prob, LANES, 2 * LANES), BF16),
        ],
        compiler_params=pltpu.CompilerParams(
            dimension_semantics=("parallel", "arbitrary"), vmem_limit_bytes=VMEM_LIMIT),
        name="delta",
    )(qkv, ba, alog, dtb)


def _ffn_kernel(x_ref, o_ref, zy_ref, hn_ref, wout_ref, gain_ref, wup_ref, cw_ref, wdown_ref,
                fin_ref, out_ref, work_a, work_b, halo, act_ref, *, tm, final):
    t = pl.program_id(1)
    works = (work_a, work_b)

    @pl.when(t == 0)
    def _():
        halo[...] = jnp.zeros_like(halo)

    ones = _group_ones(COL_BLK)
    x1 = x_ref[...] + jnp.dot(zy_ref[:, DN_WIDTH:D_MODEL], wout_ref[DN_WIDTH:D_MODEL, :],
                              preferred_element_type=F32)
    for j in range(DN_WIDTH // COL_BLK):
        sl = slice(j * COL_BLK, (j + 1) * COL_BLK)
        o = o_ref[:, sl]
        ms = _group_sum(o * o, ones) * (1.0 / HEAD_DIM)
        o_dn = o * lax.rsqrt(ms + EPS) * hn_ref[:, sl] * _silu(zy_ref[:, sl].astype(F32))
        x1 = x1 + jnp.dot(o_dn.astype(BF16), wout_ref[sl, :], preferred_element_type=F32)
    hb = _rms(x1, gain_ref[...]).astype(BF16)
    def up(j):
        return (jnp.dot(hb, wup_ref[:, j * COL_BLK:(j + 1) * COL_BLK], preferred_element_type=F32),
                jnp.dot(hb, wup_ref[:, D_FF + j * COL_BLK:D_FF + (j + 1) * COL_BLK],
                        preferred_element_type=F32))

    n_blk = D_FF // COL_BLK
    val, gate = up(0)
    for j in range(n_blk):
        nxt = up(j + 1) if j + 1 < n_blk else None
        sl = slice(j * COL_BLK, (j + 1) * COL_BLK)
        cv = _causal_conv(val, works[j % 2], halo, sl, cw_ref, 3, tm)
        act_ref[:, sl] = (_silu(cv) * gate).astype(BF16)
        if nxt is not None:
            val, gate = nxt
    x2 = x1 + jnp.dot(act_ref[...], wdown_ref[...], preferred_element_type=F32)
    if final:
        x2 = _rms(x2, fin_ref[...])
    out_ref[...] = x2


def _ffn(x, o, zy, layer, hn, w_out, gain, w_up, cw, w_down, fin, tm, final):
    b, t, _ = x.shape
    const = lambda i, j: (0, 0)
    lay = lambda i, j: (layer, 0, 0)
    tok = lambda i, j: (i, j, 0)
    once = pl.Buffered(1)
    return pl.pallas_call(
        functools.partial(_ffn_kernel, tm=tm, final=final),
        grid=(b, t // tm),
        in_specs=[
            pl.BlockSpec((None, tm, D_MODEL), tok),
            pl.BlockSpec((None, tm, DN_WIDTH), tok),
            pl.BlockSpec((None, tm, D_MODEL), tok),
            pl.BlockSpec((None, 1, DN_WIDTH), lay),
            pl.BlockSpec((None, D_MODEL, D_MODEL), lay, pipeline_mode=once),
            pl.BlockSpec((None, 1, D_MODEL), lay),
            pl.BlockSpec((None, D_MODEL, 2 * D_FF), lay, pipeline_mode=once),
            pl.BlockSpec((None, 3, D_FF), lay),
            pl.BlockSpec((None, D_FF, D_MODEL), lay, pipeline_mode=once),
            pl.BlockSpec((1, D_MODEL), const),
        ],
        out_specs=pl.BlockSpec((None, tm, D_MODEL), tok),
        out_shape=jax.ShapeDtypeStruct((b, t, D_MODEL), F32),
        scratch_shapes=[
            pltpu.VMEM((HALO + tm, COL_BLK), F32),
            pltpu.VMEM((HALO + tm, COL_BLK), F32),
            pltpu.VMEM((HALO, D_FF), F32),
            pltpu.VMEM((tm, D_FF), BF16),
        ],
        compiler_params=pltpu.CompilerParams(
            dimension_semantics=("parallel", "arbitrary"), vmem_limit_bytes=VMEM_LIMIT),
        name="ffn",
    )(x, o, zy, hn, w_out, gain, w_up, cw, w_down, fin)


def kernel(x, attn_norm, w_in, conv_qkv, a_log, dt_bias, head_norm, conv_sc, sc_norm, w_out,
           ffn_norm, w_up, conv_ffn, w_down, final_norm):
    d = x.shape[-1]
    depth = w_in.shape[0]
    ba_lo = QKV_WIDTH + DN_WIDTH
    ba_hi = ba_lo + 2 * DN_HEADS
    fin = final_norm.reshape(1, d)
    w_main = jnp.concatenate([w_in[:, :, :ba_lo], w_in[:, :, ba_hi:]], axis=2).astype(BF16)
    w_ba = jnp.pad(w_in[:, :, ba_lo:ba_hi], ((0, 0), (0, 0), (0, LANES - 2 * DN_HEADS))).astype(BF16)
    w_out_b = w_out.astype(BF16)
    w_up_b = w_up.astype(BF16)
    w_down_b = w_down.astype(BF16)
    attn_gain = attn_norm.reshape(depth, 1, d)
    ffn_gain = ffn_norm.reshape(depth, 1, d)
    sc_gain = sc_norm.reshape(depth, 1, SC_WIDTH)
    head_gain = jnp.tile(head_norm, (1, DN_HEADS)).reshape(depth, 1, DN_WIDTH)
    lane_pad = ((0, 0), (DN_HEADS, LANES - 2 * DN_HEADS))
    alog = jnp.pad(a_log, lane_pad).reshape(depth, 1, LANES)
    dtb = jnp.pad(dt_bias, lane_pad).reshape(depth, 1, LANES)
    for l in range(depth):
        qkv, zy, ba = _inproj(x, l, attn_gain, w_main, w_ba, conv_qkv, conv_sc, sc_gain,
                              tm=TOKENS_PER_STEP)
        o = _delta(qkv, ba, l, alog, dtb, ts=DELTA_TOKENS_PER_STEP)
        x = _ffn(x, o, zy, l, head_gain, w_out_b, ffn_gain, w_up_b, conv_ffn, w_down_b, fin,
                 tm=TOKENS_PER_STEP, final=(l == depth - 1))
    return x
```

```python
import functools

import jax
import jax.numpy as jnp
from jax import lax
from jax.experimental import pallas as pl
from jax.experimental.pallas import tpu as pltpu

F32 = jnp.float32
BF16 = jnp.bfloat16

D_MODEL = 1024
DN_HEADS = 8
HEAD_DIM = 64
DN_WIDTH = DN_HEADS * HEAD_DIM
SC_WIDTH = D_MODEL - DN_WIDTH
QKV_WIDTH = 3 * DN_WIDTH
D_FF = 2816
CHUNK = 64
EPS = 1e-6
LANES = 128
N_PAIRS = DN_HEADS // 2
MAIN_COLS = QKV_WIDTH + DN_WIDTH + 3 * SC_WIDTH
Z_OFF = QKV_WIDTH
SCB_OFF = Z_OFF + DN_WIDTH
SCC_OFF = SCB_OFF + SC_WIDTH
SCX_OFF = SCC_OFF + SC_WIDTH
HALO = 8
COL_BLK = 256
NEG_BIG = -1e30

VMEM_LIMIT = 56 * 1024 * 1024
TOKENS_PER_STEP = 512
DELTA_TOKENS_PER_STEP = 1024
SUB_TOKENS = 256


def _mm(a, b):
    return jnp.dot(a.astype(BF16), b.astype(BF16), preferred_element_type=F32)


def _mm_nt(a, b):
    return lax.dot_general(a.astype(BF16), b.astype(BF16), (((1,), (1,)), ((), ())),
                           preferred_element_type=F32)


def _rms(x, gain):
    return x * lax.rsqrt(jnp.mean(x * x, axis=-1, keepdims=True) + EPS) * gain


def _silu(x):
    return x * jax.nn.sigmoid(x)


def _group_ones(n):
    r = lax.broadcasted_iota(jnp.int32, (n, n), 0) // HEAD_DIM
    c = lax.broadcasted_iota(jnp.int32, (n, n), 1) // HEAD_DIM
    return (r == c).astype(BF16)


def _group_sum(x, ones):
    return jnp.dot(x.astype(BF16), ones, preferred_element_type=F32)


def _causal_conv(pre, work, halo, sl, cw_ref, width, rows):
    work[0:HALO, :] = halo[:, sl]
    work[HALO:HALO + rows, :] = pre
    acc = pre * cw_ref[width - 1:width, sl]
    for j in range(1, width):
        acc = acc + work[HALO - j:HALO - j + rows, :] * cw_ref[width - 1 - j:width - j, sl]
    halo[:, sl] = work[rows:rows + HALO, :]
    return acc


def _inproj_kernel(x_ref, gain_ref, w_ref, wba_ref, cwq_ref, cwsc_ref, scn_ref,
                   qkv_ref, zy_ref, ba_ref, work_a, work_b, halo_q, halo_s, *, tm):
    t = pl.program_id(1)

    @pl.when(t == 0)
    def _():
        halo_q[...] = jnp.zeros_like(halo_q)
        halo_s[...] = jnp.zeros_like(halo_s)

    hb = _rms(x_ref[...], gain_ref[...]).astype(BF16)
    ones = _group_ones(COL_BLK)
    works = (work_a, work_b)

    def proj(off):
        return jnp.dot(hb, w_ref[:, off:off + COL_BLK], preferred_element_type=F32)

    def qkv_epilogue(j, pre):
        sl = slice(j * COL_BLK, (j + 1) * COL_BLK)
        y = _silu(_causal_conv(pre[0], works[j % 2], halo_q, sl, cwq_ref, 4, tm))
        if j * COL_BLK < 2 * DN_WIDTH:
            y = y * lax.rsqrt(_group_sum(y * y, ones) + EPS)
            if j * COL_BLK < DN_WIDTH:
                y = y * (HEAD_DIM ** -0.5)
        qkv_ref[:, sl] = y.astype(BF16)

    def z_epilogue(j, pre):
        zy_ref[:, j * COL_BLK:(j + 1) * COL_BLK] = pre[0].astype(BF16)

    def sc_epilogue(j, pre):
        sl = slice(j * COL_BLK, (j + 1) * COL_BLK)
        y = pre[0] * _causal_conv(pre[1] * pre[2], works[j % 2], halo_s, sl, cwsc_ref, 3, tm)
        ms = _group_sum(y * y, ones) * (1.0 / HEAD_DIM)
        zy_ref[:, DN_WIDTH + j * COL_BLK:DN_WIDTH + (j + 1) * COL_BLK] = (
            y * lax.rsqrt(ms + EPS) * scn_ref[:, sl]).astype(BF16)

    def ba_epilogue(j, pre):
        ba_ref[...] = pre[0]

    pre = {}

    def p(key, off):
        return lambda: pre.__setitem__(key, proj(off))

    def p_ba():
        pre["ba"] = jnp.dot(hb, wba_ref[...], preferred_element_type=F32)

    def e(fn, j, *keys):
        return lambda: fn(j, tuple(pre.pop(k) for k in keys))

    qkv = ["q0", "q1", "k0", "k1", "v0", "v1"]
    p_qkv = {name: p(name, j * COL_BLK) for j, name in enumerate(qkv)}
    e_qkv = {name: e(qkv_epilogue, j, name) for j, name in enumerate(qkv)}
    p_z = [p("z%d" % j, Z_OFF + j * COL_BLK) for j in range(2)]
    e_z = [e(z_epilogue, j, "z%d" % j) for j in range(2)]
    p_sc = [[p("s%s%d" % (n, j), off + j * COL_BLK)
             for n, off in (("b", SCB_OFF), ("c", SCC_OFF), ("x", SCX_OFF))] for j in range(2)]
    e_sc = [e(sc_epilogue, j, "sb%d" % j, "sc%d" % j, "sx%d" % j) for j in range(2)]
    schedule = [
        p_qkv["q0"], p_qkv["q1"],
        p_sc[0][0], p_qkv["k0"], e_qkv["q0"],
        p_sc[0][1], p_qkv["k1"], e_qkv["q1"],
        p_sc[0][2], p_qkv["v0"], e_qkv["k0"],
        p_z[0], p_qkv["v1"], e_qkv["k1"],
        p_z[1], p_sc[1][0], e_sc[0], e_qkv["v0"], e_z[0],
        p_sc[1][1], p_sc[1][2], e_qkv["v1"], e_z[1],
        p_ba, e_sc[1], e(ba_epilogue, 0, "ba"),
    ]
    for emit in schedule:
        emit()


def _inproj(x, layer, gain, w_main, w_ba, cwq, cwsc, scn, tm):
    b, t, _ = x.shape
    lay = lambda i, j: (layer, 0, 0)
    once = pl.Buffered(1)
    tok = lambda i, j: (i, j, 0)
    return pl.pallas_call(
        functools.partial(_inproj_kernel, tm=tm),
        grid=(b, t // tm),
        in_specs=[
            pl.BlockSpec((None, tm, D_MODEL), tok),
            pl.BlockSpec((None, 1, D_MODEL), lay),
            pl.BlockSpec((None, D_MODEL, MAIN_COLS), lay, pipeline_mode=once),
            pl.BlockSpec((None, D_MODEL, LANES), lay, pipeline_mode=once),
            pl.BlockSpec((None, 4, QKV_WIDTH), lay),
            pl.BlockSpec((None, 3, SC_WIDTH), lay),
            pl.BlockSpec((None, 1, SC_WIDTH), lay),
        ],
        out_specs=[
            pl.BlockSpec((None, tm, QKV_WIDTH), tok),
            pl.BlockSpec((None, tm, D_MODEL), tok),
            pl.BlockSpec((None, tm, LANES), tok),
        ],
        out_shape=[
            jax.ShapeDtypeStruct((b, t, QKV_WIDTH), BF16),
            jax.ShapeDtypeStruct((b, t, D_MODEL), BF16),
            jax.ShapeDtypeStruct((b, t, LANES), F32),
        ],
        scratch_shapes=[
            pltpu.VMEM((HALO + tm, COL_BLK), F32),
            pltpu.VMEM((HALO + tm, COL_BLK), F32),
            pltpu.VMEM((HALO, QKV_WIDTH), F32),
            pltpu.VMEM((HALO, SC_WIDTH), F32),
        ],
        compiler_params=pltpu.CompilerParams(
            dimension_semantics=("parallel", "arbitrary"), vmem_limit_bytes=VMEM_LIMIT),
        name="inproj",
    )(x, gain, w_main, w_ba, cwq, cwsc, scn)


def _delta_kernel(qkv_ref, ba_ref, alog_ref, dtb_ref, o_ref, bs, gs, s_ref,
                  egl_ref, c_ref, u_ref, p_ref, qk_ref, sp_ref, kdt_ref, wq_ref, rhs_ref, *, ts):
    t = pl.program_id(1)

    @pl.when(t == 0)
    def _():
        s_ref[...] = jnp.zeros_like(s_ref)

    ba = ba_ref[...]
    bs[...] = jax.nn.sigmoid(ba)
    sp_in = ba + dtb_ref[...]
    softplus = jnp.maximum(sp_in, 0.0) + jnp.log1p(jnp.exp(-jnp.abs(sp_in)))
    g_raw = -jnp.exp(alog_ref[...]) * softplus
    rt = lax.broadcasted_iota(jnp.int32, (SUB_TOKENS, SUB_TOKENS), 0)
    ct = lax.broadcasted_iota(jnp.int32, (SUB_TOKENS, SUB_TOKENS), 1)
    tri = ((rt >= ct) & (rt // CHUNK == ct // CHUNK)).astype(BF16)
    g_hi = g_raw.astype(BF16)
    g_r1 = g_raw - g_hi.astype(F32)
    g_mid = g_r1.astype(BF16)
    g_lo = (g_r1 - g_mid.astype(F32)).astype(BF16)
    for h in range(ts // SUB_TOKENS):
        rs = slice(h * SUB_TOKENS, (h + 1) * SUB_TOKENS)
        gs[rs, :] = (jnp.dot(tri, g_hi[rs, :], preferred_element_type=F32)
                     + jnp.dot(tri, g_mid[rs, :], preferred_element_type=F32)
                     + jnp.dot(tri, g_lo[rs, :], preferred_element_type=F32))

    row = lax.broadcasted_iota(jnp.int32, (LANES, LANES), 0)
    col = lax.broadcasted_iota(jnp.int32, (LANES, LANES), 1)
    same_head = (row // CHUNK) == (col // CHUNK)
    causal = same_head & (row >= col)
    strict = same_head & (row > col)
    blk16 = (row // 16) == (col // 16)
    eye_mask = row == col
    lane_lo = lax.broadcasted_iota(jnp.int32, (CHUNK, LANES), 1) < HEAD_DIM

    def stack(x):
        zero = jnp.zeros_like(x)
        return jnp.concatenate([jnp.where(lane_lo, x, zero), jnp.where(lane_lo, zero, x)], axis=0)

    def rep_cols(x, c1, c2):
        return jnp.concatenate([jnp.broadcast_to(x[:, c1:c1 + 1], (CHUNK, LANES)),
                                jnp.broadcast_to(x[:, c2:c2 + 1], (CHUNK, LANES))], axis=0)

    n_sub = ts // SUB_TOKENS
    chunks_per_sub = SUB_TOKENS // CHUNK

    def problems(h):
        return [(h * chunks_per_sub + ci, p) for ci in range(chunks_per_sub) for p in range(N_PAIRS)]

    def prep_thunks(h, env):
        def one(c, p):
            i = c * N_PAIRS + p
            rs = slice(c * CHUNK, (c + 1) * CHUNK)
            brep = rep_cols(bs[rs, :], 2 * p, 2 * p + 1)
            g = rep_cols(gs[rs, :], DN_HEADS + 2 * p, DN_HEADS + 2 * p + 1)
            gl = jnp.concatenate(
                [jnp.broadcast_to(g[CHUNK - 1:CHUNK, :], (CHUNK, LANES)),
                 jnp.broadcast_to(g[2 * CHUNK - 1:2 * CHUNK, :], (CHUNK, LANES))], axis=0)
            q_bf = stack(qkv_ref[rs, p * LANES:(p + 1) * LANES])
            k_bf = stack(qkv_ref[rs, DN_WIDTH + p * LANES:DN_WIDTH + (p + 1) * LANES])
            v = stack(qkv_ref[rs, 2 * DN_WIDTH + p * LANES:2 * DN_WIDTH + (p + 1) * LANES])
            k = k_bf.astype(F32)
            kb = k * brep
            eg = jnp.exp(g)
            decay = jnp.exp(jnp.where(causal, g - g.T, NEG_BIG))
            egl_ref[i] = jnp.exp(gl)
            a_mat = jnp.where(strict, _mm_nt(kb, k_bf) * decay, 0.0)
            qk_ref[i] = (_mm_nt(q_bf, k_bf) * decay).astype(BF16)
            wq_ref[i, LANES:2 * LANES, :] = (q_bf.astype(F32) * eg).astype(BF16)
            kdt_ref[i] = (k * jnp.exp(gl - g)).T.astype(BF16)
            rhs_ref[i] = jnp.concatenate([v.astype(F32) * brep, kb * eg], axis=1).astype(BF16)
            env[("bm", i)] = jnp.where(blk16, -a_mat, 0.0)
            env[("lm", i)] = jnp.where(blk16, 0.0, a_mat).astype(BF16)
        return [functools.partial(one, c, p) for c, p in problems(h)]

    def chain_thunks(h, env):
        by_chunk = [[c * N_PAIRS + p for p in range(N_PAIRS)]
                    for c in range(h * chunks_per_sub, (h + 1) * chunks_per_sub)]

        def stage(fn):
            return [functools.partial(lambda ids: [fn(i) for i in ids], ids) for ids in by_chunk]

        def plus_eye(x):
            return jnp.where(eye_mask, jnp.ones_like(x), x)

        def s_b2(i):
            bm = env.pop(("bm", i)).astype(BF16)
            env[("dinv", i)] = plus_eye(bm)
            env[("b2", i)] = _mm(bm, bm).astype(BF16)

        def s_b4(i):
            b2 = env.pop(("b2", i))
            env[("b4", i)] = _mm(b2, b2).astype(BF16)
            env[("dinv", i)] = _mm(env[("dinv", i)], plus_eye(b2)).astype(BF16)

        def s_b8(i):
            b4 = env.pop(("b4", i))
            env[("b8", i)] = _mm(b4, b4).astype(BF16)
            env[("dinv", i)] = _mm(env[("dinv", i)], plus_eye(b4)).astype(BF16)

        def s_dinv(i):
            env[("dinv", i)] = _mm(env[("dinv", i)], plus_eye(env.pop(("b8", i)))).astype(BF16)

        def s_nm(i):
            env[("nm", i)] = (-_mm(env[("dinv", i)], env.pop(("lm", i)))).astype(BF16)

        def s_n2(i):
            env[("n2", i)] = _mm(env[("nm", i)], env[("nm", i)]).astype(BF16)

        def s_t1(i):
            env[("tinv", i)] = _mm(plus_eye(env.pop(("n2", i))), env.pop(("dinv", i))).astype(BF16)

        def s_t2(i):
            env[("tinv", i)] = _mm(plus_eye(env.pop(("nm", i))), env[("tinv", i)]).astype(BF16)

        def s_y(i):
            env[("y", i)] = _mm(env.pop(("tinv", i)), rhs_ref[i])

        def s_cp(i):
            y = env.pop(("y", i))
            cp = _mm(kdt_ref[i], y)
            c_ref[i] = cp[:, 0:LANES]
            p_ref[i] = cp[:, LANES:2 * LANES].astype(BF16)
            u_ref[i] = y[:, 0:LANES]
            wq_ref[i, 0:LANES, :] = y[:, LANES:2 * LANES].astype(BF16)

        out = []
        for fn in (s_b2, s_b4, s_b8, s_dinv, s_nm, s_n2, s_t1, s_t2, s_y, s_cp):
            out.extend(stage(fn))
        return out

    def tail_thunks(h, env):
        def serial(c):
            state = env["state"]
            s_bf = [s.astype(BF16) for s in state]
            for p in range(N_PAIRS):
                sp_ref[c * N_PAIRS + p] = s_bf[p]
            upd = [_mm(p_ref[c * N_PAIRS + p], s_bf[p]) for p in range(N_PAIRS)]
            env["state"] = [state[p] * egl_ref[c * N_PAIRS + p] - upd[p] + c_ref[c * N_PAIRS + p]
                            for p in range(N_PAIRS)]

        def output(c):
            idx = [c * N_PAIRS + p for p in range(N_PAIRS)]
            ws_qs = [_mm(wq_ref[i], sp_ref[i]) for i in idx]
            v_new = [u_ref[i] - x[0:LANES, :] for i, x in zip(idx, ws_qs)]
            o2 = [x[LANES:2 * LANES, :] + _mm(qk_ref[i], v) for i, x, v in zip(idx, ws_qs, v_new)]
            for p, o in enumerate(o2):
                o_ref[c * CHUNK:(c + 1) * CHUNK, p * LANES:(p + 1) * LANES] = (
                    o[0:CHUNK, :] + o[CHUNK:2 * CHUNK, :])

        chunks = range(h * chunks_per_sub, (h + 1) * chunks_per_sub)
        return ([functools.partial(serial, c) for c in chunks]
                + [functools.partial(output, c) for c in chunks])

    def interleave(*lists):
        keyed = []
        for li, thunks in enumerate(lists):
            for k, f in enumerate(thunks):
                keyed.append(((k + 0.5) / len(thunks), li, k, f))
        keyed.sort(key=lambda t: t[:3])
        return [t[3] for t in keyed]

    env = {"state": [s_ref[p] for p in range(N_PAIRS)]}
    for step in range(n_sub + 2):
        lists = []
        if 0 <= step - 1 < n_sub:
            lists.append(chain_thunks(step - 1, env))
        if step < n_sub:
            lists.append(prep_thunks(step, env))
        if 0 <= step - 2 < n_sub:
            lists.append(tail_thunks(step - 2, env))
        for thunk in interleave(*lists):
            thunk()
    for p in range(N_PAIRS):
        s_ref[p] = env["state"][p]


def _delta(qkv, ba, layer, alog, dtb, ts):
    b, t, _ = qkv.shape
    n_prob = (ts // CHUNK) * N_PAIRS
    lay = lambda i, j: (layer, 0, 0)
    tok = lambda i, j: (i, j, 0)
    return pl.pallas_call(
        functools.partial(_delta_kernel, ts=ts),
        grid=(b, t // ts),
        in_specs=[
            pl.BlockSpec((None, ts, QKV_WIDTH), tok),
            pl.BlockSpec((None, ts, LANES), tok),
            pl.BlockSpec((None, 1, LANES), lay),
            pl.BlockSpec((None, 1, LANES), lay),
        ],
        out_specs=pl.BlockSpec((None, ts, DN_WIDTH), tok),
        out_shape=jax.ShapeDtypeStruct((b, t, DN_WIDTH), F32),
        scratch_shapes=[
            pltpu.VMEM((ts, LANES), F32),
            pltpu.VMEM((ts, LANES), F32),
            pltpu.VMEM((N_PAIRS, LANES, LANES), F32),
            pltpu.VMEM((n_prob, LANES, LANES), F32),
            pltpu.VMEM((n_prob, LANES, LANES), F32),
            pltpu.VMEM((n_prob, LANES, LANES), F32),
            pltpu.VMEM((n_prob, LANES, LANES), BF16),
            pltpu.VMEM((n_prob, LANES, LANES), BF16),
            pltpu.VMEM((n_prob, LANES, LANES), BF16),
            pltpu.VMEM((n_prob, LANES, LANES), BF16),
            pltpu.VMEM((n_prob, 2 * LANES, LANES), BF16),
            pltpu.VMEM((n_prob, LANES, 2 * LANES), BF16),
        ],
        compiler_params=pltpu.CompilerParams(
            dimension_semantics=("parallel", "arbitrary"), vmem_limit_bytes=VMEM_LIMIT),
        name="delta",
    )(qkv, ba, alog, dtb)


def _ffn_kernel(x_ref, o_ref, zy_ref, hn_ref, wout_ref, gain_ref, wup_ref, cw_ref, wdown_ref,
                fin_ref, out_ref, work_a, work_b, halo, act_ref, *, tm, final):
    t = pl.program_id(1)
    works = (work_a, work_b)

    @pl.when(t == 0)
    def _():
        halo[...] = jnp.zeros_like(halo)

    ones = _group_ones(COL_BLK)
    x1 = x_ref[...] + jnp.dot(zy_ref[:, DN_WIDTH:D_MODEL], wout_ref[DN_WIDTH:D_MODEL, :],
                              preferred_element_type=F32)
    for j in range(DN_WIDTH // COL_BLK):
        sl = slice(j * COL_BLK, (j + 1) * COL_BLK)
        o = o_ref[:, sl]
        ms = _group_sum(o * o, ones) * (1.0 / HEAD_DIM)
        o_dn = o * lax.rsqrt(ms + EPS) * hn_ref[:, sl] * _silu(zy_ref[:, sl].astype(F32))
        x1 = x1 + jnp.dot(o_dn.astype(BF16), wout_ref[sl, :], preferred_element_type=F32)
    hb = (x1 * gain_ref[...]).astype(BF16)
    rowscale = jnp.broadcast_to(
        lax.rsqrt(jnp.mean(x1 * x1, axis=-1, keepdims=True) + EPS), (tm, COL_BLK))

    def up(j):
        return (jnp.dot(hb, wup_ref[:, j * COL_BLK:(j + 1) * COL_BLK],
                        preferred_element_type=F32) * rowscale,
                jnp.dot(hb, wup_ref[:, D_FF + j * COL_BLK:D_FF + (j + 1) * COL_BLK],
                        preferred_element_type=F32) * rowscale)

    n_blk = D_FF // COL_BLK
    val, gate = up(0)
    for j in range(n_blk):
        nxt = up(j + 1) if j + 1 < n_blk else None
        sl = slice(j * COL_BLK, (j + 1) * COL_BLK)
        cv = _causal_conv(val, works[j % 2], halo, sl, cw_ref, 3, tm)
        act_ref[:, sl] = (_silu(cv) * gate).astype(BF16)
        if nxt is not None:
            val, gate = nxt
    x2 = x1 + jnp.dot(act_ref[...], wdown_ref[...], preferred_element_type=F32)
    if final:
        x2 = _rms(x2, fin_ref[...])
    out_ref[...] = x2


def _ffn(x, o, zy, layer, hn, w_out, gain, w_up, cw, w_down, fin, tm, final):
    b, t, _ = x.shape
    const = lambda i, j: (0, 0)
    lay = lambda i, j: (layer, 0, 0)
    tok = lambda i, j: (i, j, 0)
    once = pl.Buffered(1)
    return pl.pallas_call(
        functools.partial(_ffn_kernel, tm=tm, final=final),
        grid=(b, t // tm),
        in_specs=[
            pl.BlockSpec((None, tm, D_MODEL), tok),
            pl.BlockSpec((None, tm, DN_WIDTH), tok),
            pl.BlockSpec((None, tm, D_MODEL), tok),
            pl.BlockSpec((None, 1, DN_WIDTH), lay),
            pl.BlockSpec((None, D_MODEL, D_MODEL), lay, pipeline_mode=once),
            pl.BlockSpec((None, 1, D_MODEL), lay),
            pl.BlockSpec((None, D_MODEL, 2 * D_FF), lay, pipeline_mode=once),
            pl.BlockSpec((None, 3, D_FF), lay),
            pl.BlockSpec((None, D_FF, D_MODEL), lay, pipeline_mode=once),
            pl.BlockSpec((1, D_MODEL), const),
        ],
        out_specs=pl.BlockSpec((None, tm, D_MODEL), tok),
        out_shape=jax.ShapeDtypeStruct((b, t, D_MODEL), F32),
        scratch_shapes=[
            pltpu.VMEM((HALO + tm, COL_BLK), F32),
            pltpu.VMEM((HALO + tm, COL_BLK), F32),
            pltpu.VMEM((HALO, D_FF), F32),
            pltpu.VMEM((tm, D_FF), BF16),
        ],
        compiler_params=pltpu.CompilerParams(
            dimension_semantics=("parallel", "arbitrary"), vmem_limit_bytes=VMEM_LIMIT),
        name="ffn",
    )(x, o, zy, hn, w_out, gain, w_up, cw, w_down, fin)


def kernel(x, attn_norm, w_in, conv_qkv, a_log, dt_bias, head_norm, conv_sc, sc_norm, w_out,
           ffn_norm, w_up, conv_ffn, w_down, final_norm):
    d = x.shape[-1]
    depth = w_in.shape[0]
    ba_lo = QKV_WIDTH + DN_WIDTH
    ba_hi = ba_lo + 2 * DN_HEADS
    fin = final_norm.reshape(1, d)
    w_main = jnp.concatenate([w_in[:, :, :ba_lo], w_in[:, :, ba_hi:]], axis=2).astype(BF16)
    w_ba = jnp.pad(w_in[:, :, ba_lo:ba_hi], ((0, 0), (0, 0), (0, LANES - 2 * DN_HEADS))).astype(BF16)
    w_out_b = w_out.astype(BF16)
    w_up_b = w_up.astype(BF16)
    w_down_b = w_down.astype(BF16)
    attn_gain = attn_norm.reshape(depth, 1, d)
    ffn_gain = ffn_norm.reshape(depth, 1, d)
    sc_gain = sc_norm.reshape(depth, 1, SC_WIDTH)
    head_gain = jnp.tile(head_norm, (1, DN_HEADS)).reshape(depth, 1, DN_WIDTH)
    lane_pad = ((0, 0), (DN_HEADS, LANES - 2 * DN_HEADS))
    alog = jnp.pad(a_log, lane_pad).reshape(depth, 1, LANES)
    dtb = jnp.pad(dt_bias, lane_pad).reshape(depth, 1, LANES)
    for l in range(depth):
        qkv, zy, ba = _inproj(x, l, attn_gain, w_main, w_ba, conv_qkv, conv_sc, sc_gain,
                              tm=TOKENS_PER_STEP)
        o = _delta(qkv, ba, l, alog, dtb, ts=DELTA_TOKENS_PER_STEP)
        x = _ffn(x, o, zy, l, head_gain, w_out_b, ffn_gain, w_up_b, conv_ffn, w_down_b, fin,
                 tm=TOKENS_PER_STEP, final=(l == depth - 1))
    return x
```

```python
import functools

import jax
import jax.numpy as jnp
from jax import lax
from jax.experimental import pallas as pl
from jax.experimental.pallas import tpu as pltpu

F32 = jnp.float32
BF16 = jnp.bfloat16

D_MODEL = 1024
DN_HEADS = 8
HEAD_DIM = 64
DN_WIDTH = DN_HEADS * HEAD_DIM
SC_WIDTH = D_MODEL - DN_WIDTH
QKV_WIDTH = 3 * DN_WIDTH
D_FF = 2816
CHUNK = 64
EPS = 1e-6
LANES = 128
N_PAIRS = DN_HEADS // 2
MAIN_COLS = QKV_WIDTH + DN_WIDTH + 3 * SC_WIDTH
Z_OFF = QKV_WIDTH
SCB_OFF = Z_OFF + DN_WIDTH
SCC_OFF = SCB_OFF + SC_WIDTH
SCX_OFF = SCC_OFF + SC_WIDTH
HALO = 8
COL_BLK = 256
NEG_BIG = -1e30

VMEM_LIMIT = 56 * 1024 * 1024
TOKENS_PER_STEP = 512
DELTA_TOKENS_PER_STEP = 1024
SUB_TOKENS = 256


def _mm(a, b):
    return jnp.dot(a.astype(BF16), b.astype(BF16), preferred_element_type=F32)


def _mm_nt(a, b):
    return lax.dot_general(a.astype(BF16), b.astype(BF16), (((1,), (1,)), ((), ())),
                           preferred_element_type=F32)


def _rms(x, gain):
    return x * lax.rsqrt(jnp.mean(x * x, axis=-1, keepdims=True) + EPS) * gain


def _silu(x):
    return x * jax.nn.sigmoid(x)


def _group_ones(n):
    r = lax.broadcasted_iota(jnp.int32, (n, n), 0) // HEAD_DIM
    c = lax.broadcasted_iota(jnp.int32, (n, n), 1) // HEAD_DIM
    return (r == c).astype(BF16)


def _group_sum(x, ones):
    return jnp.dot(x.astype(BF16), ones, preferred_element_type=F32)


def _causal_conv(pre, work, halo, sl, cw_ref, width, rows):
    work[0:HALO, :] = halo[:, sl]
    work[HALO:HALO + rows, :] = pre
    acc = pre * cw_ref[width - 1:width, sl]
    for j in range(1, width):
        acc = acc + work[HALO - j:HALO - j + rows, :] * cw_ref[width - 1 - j:width - j, sl]
    halo[:, sl] = work[rows:rows + HALO, :]
    return acc


def _inproj_kernel(x_ref, gain_ref, w_ref, wba_ref, cwq_ref, cwsc_ref, scn_ref,
                   qkv_ref, zy_ref, ba_ref, work_a, work_b, halo_q, halo_s, *, tm):
    t = pl.program_id(1)

    @pl.when(t == 0)
    def _():
        halo_q[...] = jnp.zeros_like(halo_q)
        halo_s[...] = jnp.zeros_like(halo_s)

    hb = _rms(x_ref[...], gain_ref[...]).astype(BF16)
    ones = _group_ones(COL_BLK)
    works = (work_a, work_b)

    def proj(off):
        return jnp.dot(hb, w_ref[:, off:off + COL_BLK], preferred_element_type=F32)

    def qkv_epilogue(j, pre):
        sl = slice(j * COL_BLK, (j + 1) * COL_BLK)
        y = _silu(_causal_conv(pre[0], works[j % 2], halo_q, sl, cwq_ref, 4, tm))
        if j * COL_BLK < 2 * DN_WIDTH:
            y = y * lax.rsqrt(_group_sum(y * y, ones) + EPS)
            if j * COL_BLK < DN_WIDTH:
                y = y * (HEAD_DIM ** -0.5)
        qkv_ref[:, sl] = y.astype(BF16)

    def z_epilogue(j, pre):
        zy_ref[:, j * COL_BLK:(j + 1) * COL_BLK] = pre[0].astype(BF16)

    def sc_epilogue(j, pre):
        sl = slice(j * COL_BLK, (j + 1) * COL_BLK)
        y = pre[0] * _causal_conv(pre[1] * pre[2], works[j % 2], halo_s, sl, cwsc_ref, 3, tm)
        ms = _group_sum(y * y, ones) * (1.0 / HEAD_DIM)
        zy_ref[:, DN_WIDTH + j * COL_BLK:DN_WIDTH + (j + 1) * COL_BLK] = (
            y * lax.rsqrt(ms + EPS) * scn_ref[:, sl]).astype(BF16)

    def ba_epilogue(j, pre):
        ba_ref[...] = pre[0]

    pre = {}

    def p(key, off):
        return lambda: pre.__setitem__(key, proj(off))

    def p_ba():
        pre["ba"] = jnp.dot(hb, wba_ref[...], preferred_element_type=F32)

    def e(fn, j, *keys):
        return lambda: fn(j, tuple(pre.pop(k) for k in keys))

    qkv = ["q0", "q1", "k0", "k1", "v0", "v1"]
    p_qkv = {name: p(name, j * COL_BLK) for j, name in enumerate(qkv)}
    e_qkv = {name: e(qkv_epilogue, j, name) for j, name in enumerate(qkv)}
    p_z = [p("z%d" % j, Z_OFF + j * COL_BLK) for j in range(2)]
    e_z = [e(z_epilogue, j, "z%d" % j) for j in range(2)]
    p_sc = [[p("s%s%d" % (n, j), off + j * COL_BLK)
             for n, off in (("b", SCB_OFF), ("c", SCC_OFF), ("x", SCX_OFF))] for j in range(2)]
    e_sc = [e(sc_epilogue, j, "sb%d" % j, "sc%d" % j, "sx%d" % j) for j in range(2)]
    schedule = [
        p_qkv["q0"], p_qkv["q1"],
        p_sc[0][0], p_qkv["k0"], e_qkv["q0"],
        p_sc[0][1], p_qkv["k1"], e_qkv["q1"],
        p_sc[0][2], p_qkv["v0"], e_qkv["k0"],
        p_z[0], p_qkv["v1"], e_qkv["k1"],
        p_z[1], p_sc[1][0], e_sc[0], e_qkv["v0"], e_z[0],
        p_sc[1][1], p_sc[1][2], e_qkv["v1"], e_z[1],
        p_ba, e_sc[1], e(ba_epilogue, 0, "ba"),
    ]
    for emit in schedule:
        emit()


def _inproj(x, layer, gain, w_main, w_ba, cwq, cwsc, scn, tm):
    b, t, _ = x.shape
    lay = lambda i, j: (layer, 0, 0)
    once = pl.Buffered(1)
    tok = lambda i, j: (i, j, 0)
    return pl.pallas_call(
        functools.partial(_inproj_kernel, tm=tm),
        grid=(b, t // tm),
        in_specs=[
            pl.BlockSpec((None, tm, D_MODEL), tok),
            pl.BlockSpec((None, 1, D_MODEL), lay),
            pl.BlockSpec((None, D_MODEL, MAIN_COLS), lay, pipeline_mode=once),
            pl.BlockSpec((None, D_MODEL, LANES), lay, pipeline_mode=once),
            pl.BlockSpec((None, 4, QKV_WIDTH), lay),
            pl.BlockSpec((None, 3, SC_WIDTH), lay),
            pl.BlockSpec((None, 1, SC_WIDTH), lay),
        ],
        out_specs=[
            pl.BlockSpec((None, tm, QKV_WIDTH), tok),
            pl.BlockSpec((None, tm, D_MODEL), tok),
            pl.BlockSpec((None, tm, LANES), tok),
        ],
        out_shape=[
            jax.ShapeDtypeStruct((b, t, QKV_WIDTH), BF16),
            jax.ShapeDtypeStruct((b, t, D_MODEL), BF16),
            jax.ShapeDtypeStruct((b, t, LANES), F32),
        ],
        scratch_shapes=[
            pltpu.VMEM((HALO + tm, COL_BLK), F32),
            pltpu.VMEM((HALO + tm, COL_BLK), F32),
            pltpu.VMEM((HALO, QKV_WIDTH), F32),
            pltpu.VMEM((HALO, SC_WIDTH), F32),
        ],
        compiler_params=pltpu.CompilerParams(
            dimension_semantics=("parallel", "arbitrary"), vmem_limit_bytes=VMEM_LIMIT),
        name="inproj",
    )(x, gain, w_main, w_ba, cwq, cwsc, scn)


def _delta_kernel(qkv_ref, ba_ref, alog_ref, dtb_ref, o_ref, bs, gs, s_ref,
                  egl_ref, c_ref, u_ref, p_ref, qk_ref, sp_ref, kdt_ref, wq_ref, rhs_ref, *, ts):
    t = pl.program_id(1)

    @pl.when(t == 0)
    def _():
        s_ref[...] = jnp.zeros_like(s_ref)

    ba = ba_ref[...]
    bs[...] = jax.nn.sigmoid(ba)
    sp_in = ba + dtb_ref[...]
    softplus = jnp.maximum(sp_in, 0.0) + jnp.log1p(jnp.exp(-jnp.abs(sp_in)))
    g_raw = -jnp.exp(alog_ref[...]) * softplus
    rt = lax.broadcasted_iota(jnp.int32, (SUB_TOKENS, SUB_TOKENS), 0)
    ct = lax.broadcasted_iota(jnp.int32, (SUB_TOKENS, SUB_TOKENS), 1)
    tri = ((rt >= ct) & (rt // CHUNK == ct // CHUNK)).astype(BF16)
    g_hi = g_raw.astype(BF16)
    g_r1 = g_raw - g_hi.astype(F32)
    g_mid = g_r1.astype(BF16)
    g_lo = (g_r1 - g_mid.astype(F32)).astype(BF16)
    for h in range(ts // SUB_TOKENS):
        rs = slice(h * SUB_TOKENS, (h + 1) * SUB_TOKENS)
        gs[rs, :] = (jnp.dot(tri, g_hi[rs, :], preferred_element_type=F32)
                     + jnp.dot(tri, g_mid[rs, :], preferred_element_type=F32)
                     + jnp.dot(tri, g_lo[rs, :], preferred_element_type=F32))

    row = lax.broadcasted_iota(jnp.int32, (LANES, LANES), 0)
    col = lax.broadcasted_iota(jnp.int32, (LANES, LANES), 1)
    same_head = (row // CHUNK) == (col // CHUNK)
    causal = same_head & (row >= col)
    strict = same_head & (row > col)
    blk16 = (row // 16) == (col // 16)
    eye_mask = row == col
    lane_lo = lax.broadcasted_iota(jnp.int32, (CHUNK, LANES), 1) < HEAD_DIM

    def stack(x):
        zero = jnp.zeros_like(x)
        return jnp.concatenate([jnp.where(lane_lo, x, zero), jnp.where(lane_lo, zero, x)], axis=0)

    def rep_cols(x, c1, c2):
        return jnp.concatenate([jnp.broadcast_to(x[:, c1:c1 + 1], (CHUNK, LANES)),
                                jnp.broadcast_to(x[:, c2:c2 + 1], (CHUNK, LANES))], axis=0)

    n_sub = ts // SUB_TOKENS
    chunks_per_sub = SUB_TOKENS // CHUNK

    def problems(h):
        return [(h * chunks_per_sub + ci, p) for ci in range(chunks_per_sub) for p in range(N_PAIRS)]

    def prep_thunks(h, env):
        def one(c, p):
            i = c * N_PAIRS + p
            rs = slice(c * CHUNK, (c + 1) * CHUNK)
            brep = rep_cols(bs[rs, :], 2 * p, 2 * p + 1)
            g = rep_cols(gs[rs, :], DN_HEADS + 2 * p, DN_HEADS + 2 * p + 1)
            gl = jnp.concatenate(
                [jnp.broadcast_to(g[CHUNK - 1:CHUNK, :], (CHUNK, LANES)),
                 jnp.broadcast_to(g[2 * CHUNK - 1:2 * CHUNK, :], (CHUNK, LANES))], axis=0)
            q_bf = stack(qkv_ref[rs, p * LANES:(p + 1) * LANES])
            k_bf = stack(qkv_ref[rs, DN_WIDTH + p * LANES:DN_WIDTH + (p + 1) * LANES])
            v = stack(qkv_ref[rs, 2 * DN_WIDTH + p * LANES:2 * DN_WIDTH + (p + 1) * LANES])
            k = k_bf.astype(F32)
            kb = k * brep
            eg = jnp.exp(g)
            decay = jnp.exp(jnp.where(causal, g - g.T, NEG_BIG))
            egl_ref[i] = jnp.exp(gl)
            a_mat = jnp.where(strict, _mm_nt(kb, k_bf) * decay, 0.0)
            qk_ref[i] = (_mm_nt(q_bf, k_bf) * decay).astype(BF16)
            wq_ref[i, LANES:2 * LANES, :] = (q_bf.astype(F32) * eg).astype(BF16)
            kdt_ref[i] = (k * jnp.exp(gl - g)).T.astype(BF16)
            rhs_ref[i] = jnp.concatenate([v.astype(F32) * brep, kb * eg], axis=1).astype(BF16)
            env[("bm", i)] = jnp.where(blk16, -a_mat, 0.0)
            env[("lm", i)] = jnp.where(blk16, 0.0, a_mat).astype(BF16)
        return [functools.partial(one, c, p) for c, p in problems(h)]

    def chain_thunks(h, env):
        by_chunk = [[c * N_PAIRS + p for p in range(N_PAIRS)]
                    for c in range(h * chunks_per_sub, (h + 1) * chunks_per_sub)]

        def stage(fn):
            return [functools.partial(lambda ids: [fn(i) for i in ids], ids) for ids in by_chunk]

        def plus_eye(x):
            return jnp.where(eye_mask, jnp.ones_like(x), x)

        def s_b2(i):
            bm = env.pop(("bm", i)).astype(BF16)
            env[("dinv", i)] = plus_eye(bm)
            env[("b2", i)] = _mm(bm, bm).astype(BF16)

        def s_b4(i):
            b2 = env.pop(("b2", i))
            env[("b4", i)] = _mm(b2, b2).astype(BF16)
            env[("dinv", i)] = _mm(env[("dinv", i)], plus_eye(b2)).astype(BF16)

        def s_b8(i):
            b4 = env.pop(("b4", i))
            env[("b8", i)] = _mm(b4, b4).astype(BF16)
            env[("dinv", i)] = _mm(env[("dinv", i)], plus_eye(b4)).astype(BF16)

        def s_dinv(i):
            env[("dinv", i)] = _mm(env[("dinv", i)], plus_eye(env.pop(("b8", i)))).astype(BF16)

        def s_nm(i):
            env[("nm", i)] = (-_mm(env[("dinv", i)], env.pop(("lm", i)))).astype(BF16)

        def s_n2(i):
            env[("n2", i)] = _mm(env[("nm", i)], env[("nm", i)]).astype(BF16)

        def s_t1(i):
            env[("tinv", i)] = _mm(plus_eye(env.pop(("n2", i))), env.pop(("dinv", i))).astype(BF16)

        def s_t2(i):
            env[("tinv", i)] = _mm(plus_eye(env.pop(("nm", i))), env[("tinv", i)]).astype(BF16)

        def s_y(i):
            env[("y", i)] = _mm(env.pop(("tinv", i)), rhs_ref[i])

        def s_cp(i):
            y = env.pop(("y", i))
            cp = _mm(kdt_ref[i], y)
            c_ref[i] = cp[:, 0:LANES]
            p_ref[i] = cp[:, LANES:2 * LANES].astype(BF16)
            u_ref[i] = y[:, 0:LANES]
            wq_ref[i, 0:LANES, :] = y[:, LANES:2 * LANES].astype(BF16)

        out = []
        for fn in (s_b2, s_b4, s_b8, s_dinv, s_nm, s_n2, s_t1, s_t2, s_y, s_cp):
            out.extend(stage(fn))
        return out

    def tail_thunks(h, env):
        def serial(c):
            state = env["state"]
            s_bf = [s.astype(BF16) for s in state]
            for p in range(N_PAIRS):
                sp_ref[c * N_PAIRS + p] = s_bf[p]
            upd = [_mm(p_ref[c * N_PAIRS + p], s_bf[p]) for p in range(N_PAIRS)]
            env["state"] = [state[p] * egl_ref[c * N_PAIRS + p] - upd[p] + c_ref[c * N_PAIRS + p]
                            for p in range(N_PAIRS)]

        def output(c):
            idx = [c * N_PAIRS + p for p in range(N_PAIRS)]
            ws_qs = [_mm(wq_ref[i], sp_ref[i]) for i in idx]
            v_new = [u_ref[i] - x[0:LANES, :] for i, x in zip(idx, ws_qs)]
            o2 = [x[LANES:2 * LANES, :] + _mm(qk_ref[i], v) for i, x, v in zip(idx, ws_qs, v_new)]
            for p, o in enumerate(o2):
                o_ref[c * CHUNK:(c + 1) * CHUNK, p * LANES:(p + 1) * LANES] = (
                    o[0:CHUNK, :] + o[CHUNK:2 * CHUNK, :])

        chunks = range(h * chunks_per_sub, (h + 1) * chunks_per_sub)
        return ([functools.partial(serial, c) for c in chunks]
                + [functools.partial(output, c) for c in chunks])

    def interleave(*lists):
        keyed = []
        for li, thunks in enumerate(lists):
            for k, f in enumerate(thunks):
                keyed.append(((k + 0.5) / len(thunks), li, k, f))
        keyed.sort(key=lambda t: t[:3])
        return [t[3] for t in keyed]

    env = {"state": [s_ref[p] for p in range(N_PAIRS)]}
    for step in range(n_sub + 2):
        lists = []
        if 0 <= step - 1 < n_sub:
            lists.append(chain_thunks(step - 1, env))
        if step < n_sub:
            lists.append(prep_thunks(step, env))
        if 0 <= step - 2 < n_sub:
            lists.append(tail_thunks(step - 2, env))
        for thunk in interleave(*lists):
            thunk()
    for p in range(N_PAIRS):
        s_ref[p] = env["state"][p]


def _delta(qkv, ba, layer, alog, dtb, ts):
    b, t, _ = qkv.shape
    n_prob = (ts // CHUNK) * N_PAIRS
    lay = lambda i, j: (layer, 0, 0)
    tok = lambda i, j: (i, j, 0)
    return pl.pallas_call(
        functools.partial(_delta_kernel, ts=ts),
        grid=(b, t // ts),
        in_specs=[
            pl.BlockSpec((None, ts, QKV_WIDTH), tok),
            pl.BlockSpec((None, ts, LANES), tok),
            pl.BlockSpec((None, 1, LANES), lay),
            pl.BlockSpec((None, 1, LANES), lay),
        ],
        out_specs=pl.BlockSpec((None, ts, DN_WIDTH), tok),
        out_shape=jax.ShapeDtypeStruct((b, t, DN_WIDTH), F32),
        scratch_shapes=[
            pltpu.VMEM((ts, LANES), F32),
            pltpu.VMEM((ts, LANES), F32),
            pltpu.VMEM((N_PAIRS, LANES, LANES), F32),
            pltpu.VMEM((n_prob, LANES, LANES), F32),
            pltpu.VMEM((n_prob, LANES, LANES), F32),
            pltpu.VMEM((n_prob, LANES, LANES), F32),
            pltpu.VMEM((n_prob, LANES, LANES), BF16),
            pltpu.VMEM((n_prob, LANES, LANES), BF16),
            pltpu.VMEM((n_prob, LANES, LANES), BF16),
            pltpu.VMEM((n_prob, LANES, LANES), BF16),
            pltpu.VMEM((n_prob, 2 * LANES, LANES), BF16),
            pltpu.VMEM((n_prob, LANES, 2 * LANES), BF16),
        ],
        compiler_params=pltpu.CompilerParams(
            dimension_semantics=("parallel", "arbitrary"), vmem_limit_bytes=VMEM_LIMIT),
        name="delta",
    )(qkv, ba, alog, dtb)


def _ffn_kernel(x_ref, o_ref, zy_ref, hn_ref, wout_ref, gain_ref, wup_ref, cw_ref, wdown_ref,
                fin_ref, out_ref, work_a, work_b, halo, act_ref, *, tm, final):
    t = pl.program_id(1)
    works = (work_a, work_b)

    @pl.when(t == 0)
    def _():
        halo[...] = jnp.zeros_like(halo)

    ones = _group_ones(COL_BLK)
    x1 = x_ref[...] + jnp.dot(zy_ref[:, DN_WIDTH:D_MODEL], wout_ref[DN_WIDTH:D_MODEL, :],
                              preferred_element_type=F32)
    for j in range(DN_WIDTH // COL_BLK):
        sl = slice(j * COL_BLK, (j + 1) * COL_BLK)
        o = o_ref[:, sl]
        ms = _group_sum(o * o, ones) * (1.0 / HEAD_DIM)
        o_dn = o * lax.rsqrt(ms + EPS) * hn_ref[:, sl] * _silu(zy_ref[:, sl].astype(F32))
        x1 = x1 + jnp.dot(o_dn.astype(BF16), wout_ref[sl, :], preferred_element_type=F32)
    hb = (x1 * gain_ref[...]).astype(BF16)
    rowscale = jnp.broadcast_to(
        lax.rsqrt(jnp.mean(x1 * x1, axis=-1, keepdims=True) + EPS), (tm, COL_BLK))

    def up(j):
        return (jnp.dot(hb, wup_ref[:, j * COL_BLK:(j + 1) * COL_BLK],
                        preferred_element_type=F32) * rowscale,
                jnp.dot(hb, wup_ref[:, D_FF + j * COL_BLK:D_FF + (j + 1) * COL_BLK],
                        preferred_element_type=F32) * rowscale)

    n_blk = D_FF // COL_BLK
    val, gate = up(0)
    for j in range(n_blk):
        nxt = up(j + 1) if j + 1 < n_blk else None
        sl = slice(j * COL_BLK, (j + 1) * COL_BLK)
        cv = _causal_conv(val, works[j % 2], halo, sl, cw_ref, 3, tm)
        act_ref[:, sl] = (_silu(cv) * gate).astype(BF16)
        if nxt is not None:
            val, gate = nxt
    x2 = x1 + jnp.dot(act_ref[...], wdown_ref[...], preferred_element_type=F32)
    if final:
        x2 = _rms(x2, fin_ref[...])
    out_ref[...] = x2


def _ffn(x, o, zy, layer, hn, w_out, gain, w_up, cw, w_down, fin, tm, final):
    b, t, _ = x.shape
    const = lambda i, j: (0, 0)
    lay = lambda i, j: (layer, 0, 0)
    tok = lambda i, j: (i, j, 0)
    once = pl.Buffered(1)
    return pl.pallas_call(
        functools.partial(_ffn_kernel, tm=tm, final=final),
        grid=(b, t // tm),
        in_specs=[
            pl.BlockSpec((None, tm, D_MODEL), tok),
            pl.BlockSpec((None, tm, DN_WIDTH), tok),
            pl.BlockSpec((None, tm, D_MODEL), tok),
            pl.BlockSpec((None, 1, DN_WIDTH), lay),
            pl.BlockSpec((None, D_MODEL, D_MODEL), lay, pipeline_mode=once),
            pl.BlockSpec((None, 1, D_MODEL), lay),
            pl.BlockSpec((None, D_MODEL, 2 * D_FF), lay, pipeline_mode=once),
            pl.BlockSpec((None, 3, D_FF), lay),
            pl.BlockSpec((None, D_FF, D_MODEL), lay, pipeline_mode=once),
            pl.BlockSpec((1, D_MODEL), const),
        ],
        out_specs=pl.BlockSpec((None, tm, D_MODEL), tok),
        out_shape=jax.ShapeDtypeStruct((b, t, D_MODEL), F32),
        scratch_shapes=[
            pltpu.VMEM((HALO + tm, COL_BLK), F32),
            pltpu.VMEM((HALO + tm, COL_BLK), F32),
            pltpu.VMEM((HALO, D_FF), F32),
            pltpu.VMEM((tm, D_FF), BF16),
        ],
        compiler_params=pltpu.CompilerParams(
            dimension_semantics=("parallel", "arbitrary"), vmem_limit_bytes=VMEM_LIMIT),
        name="ffn",
    )(x, o, zy, hn, w_out, gain, w_up, cw, w_down, fin)


def kernel(x, attn_norm, w_in, conv_qkv, a_log, dt_bias, head_norm, conv_sc, sc_norm, w_out,
           ffn_norm, w_up, conv_ffn, w_down, final_norm):
    d = x.shape[-1]
    depth = w_in.shape[0]
    ba_lo = QKV_WIDTH + DN_WIDTH
    ba_hi = ba_lo + 2 * DN_HEADS
    fin = final_norm.reshape(1, d)
    w_main = jnp.concatenate([w_in[:, :, :ba_lo], w_in[:, :, ba_hi:]], axis=2).astype(BF16)
    w_ba = jnp.pad(w_in[:, :, ba_lo:ba_hi], ((0, 0), (0, 0), (0, LANES - 2 * DN_HEADS))).astype(BF16)
    w_out_b = w_out.astype(BF16)
    w_up_b = w_up.astype(BF16)
    w_down_b = w_down.astype(BF16)
    attn_gain = attn_norm.reshape(depth, 1, d)
    ffn_gain = ffn_norm.reshape(depth, 1, d)
    sc_gain = sc_norm.reshape(depth, 1, SC_WIDTH)
    head_gain = jnp.tile(head_norm, (1, DN_HEADS)).reshape(depth, 1, DN_WIDTH)
    lane_pad = ((0, 0), (DN_HEADS, LANES - 2 * DN_HEADS))
    alog = jnp.pad(a_log, lane_pad).reshape(depth, 1, LANES)
    dtb = jnp.pad(dt_bias, lane_pad).reshape(depth, 1, LANES)
    for l in range(depth):
        qkv, zy, ba = _inproj(x, l, attn_gain, w_main, w_ba, conv_qkv, conv_sc, sc_gain,
                              tm=2 * TOKENS_PER_STEP)
        o = _delta(qkv, ba, l, alog, dtb, ts=DELTA_TOKENS_PER_STEP)
        x = _ffn(x, o, zy, l, head_gain, w_out_b, ffn_gain, w_up_b, conv_ffn, w_down_b, fin,
                 tm=TOKENS_PER_STEP, final=(l == depth - 1))
    return x
```
